```python
import math
import jax, jax.numpy as jnp
from jax import lax
import numpy as np

D_MODEL = 2048
BATCH = 2
SEQ = 8192
DEPTH = 1

SWA_Q_HEADS = 16
SWA_KV_HEADS = 2
SWA_HEAD_DIM = 64
SWA_WINDOW = 128
GDN_HEADS = 4
GDN_HEAD_DIM = 128
GDN_CONV = 4
GDN_CHUNK = 64
N_MEM = 256
XA_HEADS = 4
XA_HEAD_DIM = 128
D_FF = 4 * D_MODEL
N_BRANCH = 3
RMS_EPS = 1e-6
L2_EPS = 1e-6

SWA_Q_W = SWA_Q_HEADS * SWA_HEAD_DIM
SWA_KV_W = SWA_KV_HEADS * SWA_HEAD_DIM
GDN_W = GDN_HEADS * GDN_HEAD_DIM
XA_W = XA_HEADS * XA_HEAD_DIM
IN_SPLITS = (SWA_Q_W, SWA_KV_W, SWA_KV_W, 3 * GDN_W, GDN_HEADS, GDN_HEADS, GDN_W, XA_W, N_BRANCH * D_MODEL)
IN_WIDTH = sum(IN_SPLITS)

kernel_name = 'hybrid_swa_sink_gdn_memxattn_relu2_block'


def rms_norm(x, g):
    xf = x.astype(jnp.float32)
    y = xf * lax.rsqrt(jnp.mean(xf * xf, axis=-1, keepdims=True) + RMS_EPS)
    return (y * g.astype(jnp.float32)).astype(x.dtype)


def l2_norm(x):
    return x * lax.rsqrt(jnp.sum(x * x, axis=-1, keepdims=True) + L2_EPS)


def split_cols(t, sizes):
    idx, acc = [], 0
    for s in sizes[:-1]:
        acc += s
        idx.append(acc)
    return jnp.split(t, idx, axis=-1)


def sliding_window_attention(q, k, v, sinks):
    B, S, HQ, hd = q.shape
    HKV = k.shape[2]
    G = HQ // HKV
    W = SWA_WINDOW
    nb = S // W
    qb = q.reshape(B, nb, W, HKV, G, hd)
    kb = k.reshape(B, nb, W, HKV, hd)
    vb = v.reshape(B, nb, W, HKV, hd)

    def with_prev(t):
        prev = jnp.pad(t, ((0, 0), (1, 0), (0, 0), (0, 0), (0, 0)))[:, :-1]
        return jnp.concatenate([prev, t], axis=2)

    kc, vc = with_prev(kb), with_prev(vb)
    s = jnp.einsum('bnqhgd,bnkhd->bnhgqk', qb, kc).astype(jnp.float32) * (hd ** -0.5)
    qi = jnp.arange(W)[:, None]
    kj = jnp.arange(2 * W)[None, :]
    band = (kj > qi) & (kj <= qi + W)
    not_pad = (jnp.arange(nb)[:, None, None] > 0) | (kj >= W)[None]
    mask = band[None] & not_pad
    s = jnp.where(mask[None, :, None, None], s, -jnp.inf)
    sk = sinks.astype(jnp.float32).reshape(HKV, G)[None, None, :, :, None, None]
    m = jnp.maximum(jnp.max(s, axis=-1, keepdims=True), sk)
    p = jnp.exp(s - m)
    denom = jnp.sum(p, axis=-1, keepdims=True) + jnp.exp(sk - m)
    pr = (p / denom).astype(v.dtype)
    o = jnp.einsum('bnhgqk,bnkhd->bnqhgd', pr, vc)
    return o.reshape(B, S, HQ * hd)


def causal_depthwise_conv(x, w):
    K, C = w.shape
    return lax.conv_general_dilated(
        x, w.reshape(K, 1, C), window_strides=(1,), padding=[(K - 1, 0)],
        dimension_numbers=('NWC', 'WIO', 'NWC'), feature_group_count=C)


def chunked_gated_delta_rule(q, k, v, g, beta):
    B, S, H, dk = q.shape
    dv = v.shape[-1]
    C = GDN_CHUNK
    N = S // C

    def to_chunks(t):
        return t.reshape(B, N, C, H, -1).transpose(1, 0, 3, 2, 4)

    qc, kc, vc = to_chunks(q), to_chunks(k), to_chunks(v)
    gc = g.reshape(B, N, C, H).transpose(1, 0, 3, 2)
    bc = beta.reshape(B, N, C, H).transpose(1, 0, 3, 2)
    gcum = jnp.cumsum(gc, axis=-1)
    causal = jnp.tril(jnp.ones((C, C), dtype=bool))
    strict = jnp.tril(jnp.ones((C, C), dtype=bool), k=-1)
    decay = jnp.exp(jnp.where(causal, gcum[..., :, None] - gcum[..., None, :], -jnp.inf))
    kk = jnp.einsum('nbhcd,nbhed->nbhce', kc, kc)
    lower = jnp.where(strict, bc[..., :, None] * kk * decay, 0.0)
    a_mat = jnp.eye(C, dtype=q.dtype) + lower
    rhs = jnp.concatenate([vc * bc[..., None], kc * (bc * jnp.exp(gcum))[..., None]], axis=-1)
    sol = lax.linalg.triangular_solve(a_mat, rhs, left_side=True, lower=True, unit_diagonal=True)
    u, w = sol[..., :dv], sol[..., dv:]
    qk = jnp.einsum('nbhcd,nbhed->nbhce', qc, kc) * decay
    q_dec = qc * jnp.exp(gcum)[..., None]
    k_dec = kc * jnp.exp(gcum[..., -1:] - gcum)[..., None]
    g_last = jnp.exp(gcum[..., -1])

    def step(state, inp):
        qk_i, qd_i, kd_i, u_i, w_i, gl_i = inp
        v_new = u_i - jnp.einsum('bhcd,bhde->bhce', w_i, state)
        o = jnp.einsum('bhcd,bhde->bhce', qd_i, state) + jnp.einsum('bhce,bhef->bhcf', qk_i, v_new)
        state = state * gl_i[..., None, None] + jnp.einsum('bhcd,bhce->bhde', kd_i, v_new)
        return state, o

    state0 = jnp.zeros((B, H, dk, dv), dtype=q.dtype)
    _, o = lax.scan(step, state0, (qk, q_dec, k_dec, u, w, g_last))
    return o.transpose(1, 0, 3, 2, 4).reshape(B, S, H, dv)


def gated_deltanet(qkv, a, b, z, conv_w, a_log, dt_bias, norm_w):
    B, S, _ = qkv.shape
    H, dh = GDN_HEADS, GDN_HEAD_DIM
    f32 = jnp.float32
    qkv = jax.nn.silu(causal_depthwise_conv(qkv, conv_w))
    q, k, v = jnp.split(qkv, 3, axis=-1)
    q = l2_norm(q.reshape(B, S, H, dh).astype(f32)) * (dh ** -0.5)
    k = l2_norm(k.reshape(B, S, H, dh).astype(f32))
    v = v.reshape(B, S, H, dh).astype(f32)
    beta = jax.nn.sigmoid(b.astype(f32))
    g = -jnp.exp(a_log.astype(f32)) * jax.nn.softplus(a.astype(f32) + dt_bias.astype(f32))
    o = chunked_gated_delta_rule(q, k, v, g, beta)
    o = rms_norm(o, norm_w) * jax.nn.silu(z.reshape(B, S, H, dh).astype(f32))
    return o.reshape(B, S, H * dh).astype(qkv.dtype)


def memory_cross_attention(q, mkv):
    B, S, _ = q.shape
    q = q.reshape(B, S, XA_HEADS, XA_HEAD_DIM)
    mk, mv = jnp.split(mkv, 2, axis=-1)
    mk = mk.reshape(B, N_MEM, XA_HEADS, XA_HEAD_DIM)
    mv = mv.reshape(B, N_MEM, XA_HEADS, XA_HEAD_DIM)
    s = jnp.einsum('bshd,bmhd->bhsm', q, mk).astype(jnp.float32) * (XA_HEAD_DIM ** -0.5)
    p = jax.nn.softmax(s, axis=-1).astype(mv.dtype)
    return jnp.einsum('bhsm,bmhd->bshd', p, mv).reshape(B, S, XA_W)


def setup_inputs(seed: int = 0) -> dict:
    key = jax.random.key(seed)
    ks = jax.random.split(key, 20)
    f32 = jnp.float32
    L, D = DEPTH, D_MODEL

    def nrm(k, shape, scale):
        return jax.random.normal(k, shape, f32) * scale

    x = nrm(ks[0], (BATCH, SEQ, D), 1.0)
    mem = nrm(ks[1], (BATCH, N_MEM, D), 1.0)
    g_mix = 1.0 + nrm(ks[2], (L, D), 0.02)
    w_in = nrm(ks[3], (L, D, IN_WIDTH), D ** -0.5)
    sinks = nrm(ks[4], (L, SWA_Q_HEADS), 0.5)
    conv_w = nrm(ks[5], (L, GDN_CONV, 3 * GDN_W), GDN_CONV ** -0.5)
    a_log = jnp.log(jax.random.uniform(ks[6], (L, GDN_HEADS), f32, 1.0, 16.0))
    dt = jnp.exp(jax.random.uniform(ks[7], (L, GDN_HEADS), f32, math.log(1e-3), math.log(1e-1)))
    dt_bias = dt + jnp.log(-jnp.expm1(-dt))
    gdn_norm_w = 1.0 + nrm(ks[8], (L, GDN_HEAD_DIM), 0.02)
    g_mem = 1.0 + nrm(ks[9], (L, D), 0.02)
    w_mem_kv = nrm(ks[10], (L, D, 2 * XA_W), D ** -0.5)
    w_swa_up = nrm(ks[11], (L, SWA_Q_W, D), SWA_Q_W ** -0.5)
    w_gdn_up = nrm(ks[12], (L, GDN_W, D), GDN_W ** -0.5)
    w_xa_up = nrm(ks[13], (L, XA_W, D), XA_W ** -0.5)
    w_out = nrm(ks[14], (L, D, D), D ** -0.5)
    g_mlp = 1.0 + nrm(ks[15], (L, D), 0.02)
    w_mlp_in = nrm(ks[16], (L, D, D_FF), D ** -0.5)
    w_mlp_out = nrm(ks[17], (L, D_FF, D), D_FF ** -0.5)
    g_final = 1.0 + nrm(ks[18], (D,), 0.02)
    return {'x': x, 'mem': mem, 'g_mix': g_mix, 'w_in': w_in, 'sinks': sinks, 'conv_w': conv_w,
            'a_log': a_log, 'dt_bias': dt_bias, 'gdn_norm_w': gdn_norm_w, 'g_mem': g_mem,
            'w_mem_kv': w_mem_kv, 'w_swa_up': w_swa_up, 'w_gdn_up': w_gdn_up, 'w_xa_up': w_xa_up,
            'w_out': w_out, 'g_mlp': g_mlp, 'w_mlp_in': w_mlp_in, 'w_mlp_out': w_mlp_out,
            'g_final': g_final}


def reference(x, mem, g_mix, w_in, sinks, conv_w, a_log, dt_bias, gdn_norm_w, g_mem, w_mem_kv,
              w_swa_up, w_gdn_up, w_xa_up, w_out, g_mlp, w_mlp_in, w_mlp_out, g_final):
    B, S, D = x.shape
    h = x
    for l in range(DEPTH):
        n = rms_norm(h, g_mix[l])
        p = n @ w_in[l]
        q_a, k_a, v_a, qkv_b, a_b, b_b, z_b, q_c, gate_logits = split_cols(p, IN_SPLITS)
        y_a = sliding_window_attention(
            q_a.reshape(B, S, SWA_Q_HEADS, SWA_HEAD_DIM),
            k_a.reshape(B, S, SWA_KV_HEADS, SWA_HEAD_DIM),
            v_a.reshape(B, S, SWA_KV_HEADS, SWA_HEAD_DIM), sinks[l])
        y_b = gated_deltanet(qkv_b, a_b, b_b, z_b, conv_w[l], a_log[l], dt_bias[l], gdn_norm_w[l])
        mkv = rms_norm(mem, g_mem[l]) @ w_mem_kv[l]
        y_c = memory_cross_attention(q_c, mkv)
        g_a, g_b, g_c = jnp.split(jax.nn.sigmoid(gate_logits), N_BRANCH, axis=-1)
        merged = g_a * (y_a @ w_swa_up[l]) + g_b * (y_b @ w_gdn_up[l]) + g_c * (y_c @ w_xa_up[l])
        h = h + merged @ w_out[l]
        u = rms_norm(h, g_mlp[l]) @ w_mlp_in[l]
        h = h + jnp.square(jax.nn.relu(u)) @ w_mlp_out[l]
    return rms_norm(h, g_final)
```

```python
import functools

import jax
import jax.numpy as jnp
from jax import lax
from jax.experimental import pallas as pl
from jax.experimental.pallas import tpu as pltpu

F32 = jnp.float32
BF16 = jnp.bfloat16

D_MODEL = 2048
SWA_Q_HEADS = 16
SWA_KV_HEADS = 2
SWA_HEAD_DIM = 64
SWA_WINDOW = 128
GDN_HEADS = 4
GDN_HEAD_DIM = 128
GDN_CONV = 4
GDN_CHUNK = 64
N_MEM = 256
XA_HEADS = 4
XA_HEAD_DIM = 128
D_FF = 4 * D_MODEL
RMS_EPS = 1e-6
L2_EPS = 1e-6

SWA_Q_W = SWA_Q_HEADS * SWA_HEAD_DIM
SWA_KV_W = SWA_KV_HEADS * SWA_HEAD_DIM
GDN_W = GDN_HEADS * GDN_HEAD_DIM
XA_W = XA_HEADS * XA_HEAD_DIM

LANES = 128

P_GATE = 0
P_QA = P_GATE + 3 * D_MODEL
P_QB = P_QA + SWA_Q_W
P_KB = P_QB + GDN_W
P_VB = P_KB + GDN_W
P_Z = P_VB + GDN_W
P_QC = P_Z + GDN_W
P_KA = P_QC + XA_W
P_VA = P_KA + SWA_KV_W
P_AB = P_VA + SWA_KV_W
P_END = P_AB + LANES
IN_PROJ_TN = 1024
P_WIDTH = -(-P_END // IN_PROJ_TN) * IN_PROJ_TN

VMEM_LIMIT = 56 * 1024 * 1024


def _rms(x, g):
    return x * lax.rsqrt(jnp.mean(x * x, axis=-1, keepdims=True) + RMS_EPS) * g


def _dot(a, b):
    return jnp.dot(a, b, preferred_element_type=F32)


def _dot_nt(a, b):
    return lax.dot_general(a, b, (((1,), (1,)), ((), ())), preferred_element_type=F32)


def _dot_f32(a, b):
    return jnp.dot(a, b, preferred_element_type=F32, precision=lax.Precision.HIGHEST)


def _in_proj_kernel(x_ref, g_ref, w_ref, o_ref, n_ref):
    @pl.when(pl.program_id(1) == 0)
    def _():
        n_ref[...] = _rms(x_ref[...], g_ref[...]).astype(BF16)

    o_ref[...] = _dot(n_ref[...], w_ref[...])


def _in_proj(x, g, w, *, tm=512, tn=IN_PROJ_TN):
    t, d = x.shape
    n = w.shape[1]
    return pl.pallas_call(
        _in_proj_kernel,
        out_shape=jax.ShapeDtypeStruct((t, n), F32),
        grid=(t // tm, n // tn),
        in_specs=[
            pl.BlockSpec((tm, d), lambda i, j: (i, 0)),
            pl.BlockSpec((1, d), lambda i, j: (0, 0)),
            pl.BlockSpec((d, tn), lambda i, j: (0, j)),
        ],
        out_specs=pl.BlockSpec((tm, tn), lambda i, j: (i, j)),
        scratch_shapes=[pltpu.VMEM((tm, d), BF16)],
        compiler_params=pltpu.CompilerParams(
            dimension_semantics=("arbitrary", "arbitrary"), vmem_limit_bytes=VMEM_LIMIT),
        name="in_proj",
    )(x, g, w)


def _mem_kv_kernel(m_ref, g_ref, w_ref, o_ref):
    o_ref[...] = _dot(_rms(m_ref[...], g_ref[...]).astype(BF16), w_ref[...])


def _mem_kv(mem, g, w):
    t, d = mem.shape
    n = w.shape[1]
    return pl.pallas_call(
        _mem_kv_kernel,
        out_shape=jax.ShapeDtypeStruct((t, n), F32),
        grid=(t // N_MEM,),
        in_specs=[
            pl.BlockSpec((N_MEM, d), lambda i: (i, 0)),
            pl.BlockSpec((1, d), lambda i: (0, 0)),
            pl.BlockSpec((d, n), lambda i: (0, 0)),
        ],
        out_specs=pl.BlockSpec((N_MEM, n), lambda i: (i, 0)),
        compiler_params=pltpu.CompilerParams(
            dimension_semantics=("arbitrary",), vmem_limit_bytes=VMEM_LIMIT),
        name="mem_kv",
    )(mem, g, w)


def _swa_kernel(sinks_ref, q_ref, kc_ref, vc_ref, kp_ref, vp_ref, o_ref, *, blocks_per_seq):
    w = SWA_WINDOW
    hd = SWA_HEAD_DIM
    first = (pl.program_id(0) % blocks_per_seq) == 0
    k = jnp.concatenate([kp_ref[...], kc_ref[...]], axis=0).astype(BF16)
    v = jnp.concatenate([vp_ref[...], vc_ref[...]], axis=0).astype(BF16)
    qi = lax.broadcasted_iota(jnp.int32, (w, 2 * w), 0)
    kj = lax.broadcasted_iota(jnp.int32, (w, 2 * w), 1)
    kj_min = jnp.where(first, w, 0)
    valid = (kj > qi) & (kj <= qi + w) & (kj >= kj_min)
    group = SWA_Q_HEADS // SWA_KV_HEADS
    for h in range(SWA_Q_HEADS):
        hk = h // group
        kh = k[:, hk * hd:(hk + 1) * hd]
        vh = v[:, hk * hd:(hk + 1) * hd]
        qh = q_ref[:, h * hd:(h + 1) * hd].astype(BF16)
        s = _dot_nt(qh, kh) * (hd ** -0.5)
        s = jnp.where(valid, s, -jnp.inf)
        sk = sinks_ref[h]
        m = jnp.maximum(jnp.max(s, axis=-1, keepdims=True), sk)
        p = jnp.exp(s - m)
        denom = jnp.sum(p, axis=-1, keepdims=True) + jnp.exp(sk - m)
        pr = (p / denom).astype(BF16)
        o_ref[:, h * hd:(h + 1) * hd] = _dot(pr, vh).astype(o_ref.dtype)


def _swa(p, sinks, seq):
    t = p.shape[0]
    w = SWA_WINDOW
    blocks_per_seq = seq // w
    q_blk = P_QA // SWA_Q_W
    k_blk = P_KA // SWA_KV_W
    v_blk = P_VA // SWA_KV_W

    def prev(i):
        return jnp.maximum(i - 1, 0)

    return pl.pallas_call(
        functools.partial(_swa_kernel, blocks_per_seq=blocks_per_seq),
        out_shape=jax.ShapeDtypeStruct((t, SWA_Q_W), BF16),
        grid=(t // w,),
        in_specs=[
            pl.BlockSpec(memory_space=pltpu.SMEM),
            pl.BlockSpec((w, SWA_Q_W), lambda i: (i, q_blk)),
            pl.BlockSpec((w, SWA_KV_W), lambda i: (i, k_blk)),
            pl.BlockSpec((w, SWA_KV_W), lambda i: (i, v_blk)),
            pl.BlockSpec((w, SWA_KV_W), lambda i: (prev(i), k_blk)),
            pl.BlockSpec((w, SWA_KV_W), lambda i: (prev(i), v_blk)),
        ],
        out_specs=pl.BlockSpec((w, SWA_Q_W), lambda i: (i, 0)),
        compiler_params=pltpu.CompilerParams(dimension_semantics=("arbitrary",)),
        name="swa",
    )(sinks, p, p, p, p, p)


def _xattn_kernel(q_ref, mk_ref, mv_ref, o_ref):
    hd = XA_HEAD_DIM
    for h in range(XA_HEADS):
        sl = slice(h * hd, (h + 1) * hd)
        s = _dot_nt(q_ref[:, sl].astype(BF16), mk_ref[:, sl].astype(BF16)) * (hd ** -0.5)
        m = jnp.max(s, axis=-1, keepdims=True)
        e = jnp.exp(s - m)
        pr = (e / jnp.sum(e, axis=-1, keepdims=True)).astype(BF16)
        o_ref[:, sl] = _dot(pr, mv_ref[:, sl].astype(BF16)).astype(o_ref.dtype)


def _xattn(p, mkv, seq, *, tq=512):
    t = p.shape[0]
    tiles_per_seq = seq // tq
    q_blk = P_QC // XA_W
    return pl.pallas_call(
        _xattn_kernel,
        out_shape=jax.ShapeDtypeStruct((t, XA_W), BF16),
        grid=(t // tq,),
        in_specs=[
            pl.BlockSpec((tq, XA_W), lambda i: (i, q_blk)),
            pl.BlockSpec((N_MEM, XA_W), lambda i: (i // tiles_per_seq, 0)),
            pl.BlockSpec((N_MEM, XA_W), lambda i: (i // tiles_per_seq, 1)),
        ],
        out_specs=pl.BlockSpec((tq, XA_W), lambda i: (i, 0)),
        compiler_params=pltpu.CompilerParams(dimension_semantics=("arbitrary",)),
        name="xattn",
    )(p, mkv, mkv)


GDN_TS = 256
CONV_PAD = 8


def _gdn_kernel(qb_ref, kb_ref, vb_ref, z_ref, ab_ref, cw_ref, alog_ref, dtb_ref, nw_ref, y_ref,
                state_ref, carry_ref, xpad_ref, cv_ref, g_ref, beta_ref):
    ts = GDN_TS
    c = GDN_CHUNK
    dh = GDN_HEAD_DIM
    gw = GDN_W

    @pl.when(pl.program_id(1) == 0)
    def _():
        state_ref[...] = jnp.zeros_like(state_ref)
        carry_ref[...] = jnp.zeros_like(carry_ref)

    xpad_ref[0:CONV_PAD, :] = carry_ref[...]
    xpad_ref[CONV_PAD:, 0:gw] = qb_ref[...]
    xpad_ref[CONV_PAD:, gw:2 * gw] = kb_ref[...]
    xpad_ref[CONV_PAD:, 2 * gw:3 * gw] = vb_ref[...]
    carry_ref[...] = xpad_ref[ts:ts + CONV_PAD, :]
    for cb in range(3 * gw // LANES):
        cs = slice(cb * LANES, (cb + 1) * LANES)
        acc = None
        for i in range(GDN_CONV):
            off = CONV_PAD - (GDN_CONV - 1) + i
            term = cw_ref[i:i + 1, cs] * xpad_ref[off:off + ts, cs]
            acc = term if acc is None else acc + term
        cv_ref[:, cs] = acc * jax.nn.sigmoid(acc)

    ab = ab_ref[...]
    g_ref[...] = -jnp.exp(alog_ref[...]) * jax.nn.softplus(ab + dtb_ref[...])
    beta_ref[...] = jax.nn.sigmoid(ab)

    ri = lax.broadcasted_iota(jnp.int32, (c, c), 0)
    ci = lax.broadcasted_iota(jnp.int32, (c, c), 1)
    causal = ri >= ci
    strict = ri > ci
    tril = causal.astype(F32)

    def chunk_body(ic, carry):
        r0 = pl.multiple_of(ic * c, c)
        rows = pl.ds(r0, c)
        gcum = _dot_f32(tril, g_ref[rows, :])
        gcum_t = gcum.T
        beta_all = beta_ref[rows, :]
        for h in range(GDN_HEADS):
            hs = slice(h * dh, (h + 1) * dh)
            q = cv_ref[rows, h * dh:(h + 1) * dh]
            k = cv_ref[rows, gw + h * dh:gw + (h + 1) * dh]
            v = cv_ref[rows, 2 * gw + h * dh:2 * gw + (h + 1) * dh]
            q = q * lax.rsqrt(jnp.sum(q * q, axis=-1, keepdims=True) + L2_EPS) * (dh ** -0.5)
            k = k * lax.rsqrt(jnp.sum(k * k, axis=-1, keepdims=True) + L2_EPS)
            gc = gcum[:, h:h + 1]
            gr = gcum_t[h:h + 1, :]
            beta = beta_all[:, GDN_HEADS + h:GDN_HEADS + h + 1]
            decay = jnp.exp(jnp.where(causal, gc - gr, -jnp.inf))
            eg = jnp.exp(gc)
            g_last = gcum[c - 1:c, h:h + 1]
            kb16 = k.astype(BF16)
            kk = _dot_nt(kb16, kb16)
            qk = _dot_nt(q.astype(BF16), kb16) * decay
            m = -jnp.where(strict, beta * kk * decay, 0.0)
            sol = jnp.concatenate([v * beta, k * (beta * eg)], axis=-1)
            n_rounds = c.bit_length() - 1
            for r in range(n_rounds):
                sol = sol + _dot_f32(m, sol)
                if r + 1 < n_rounds:
                    m = _dot_f32(m, m)
            u = sol[:, :dh]
            w = sol[:, dh:]
            q_dec = q * eg
            k_dec = k * jnp.exp(g_last - gc)
            s = state_ref[h]
            ws = _dot(jnp.concatenate([w, q_dec], axis=0).astype(BF16), s.astype(BF16))
            v_new = u - ws[:c]
            o = ws[c:] + _dot(qk.astype(BF16), v_new.astype(BF16))
            state_ref[h] = s * jnp.exp(g_last) + _dot(k_dec.T.astype(BF16), v_new.astype(BF16))
            o = _rms(o, nw_ref[...])
            z = z_ref[rows, hs]
            y_ref[rows, hs] = (o * (z * jax.nn.sigmoid(z))).astype(y_ref.dtype)
        return carry

    lax.fori_loop(0, ts // c, chunk_body, 0)


def _gdn(p, conv_w, alog_row, dtb_row, norm_w, batch, seq):
    t = p.shape[0]
    ts = GDN_TS
    nj = seq // ts
    gw = GDN_W

    def rows(col_blk):
        return lambda b, j: (b * nj + j, col_blk)

    return pl.pallas_call(
        _gdn_kernel,
        out_shape=jax.ShapeDtypeStruct((t, gw), BF16),
        grid=(batch, nj),
        in_specs=[
            pl.BlockSpec((ts, gw), rows(P_QB // gw)),
            pl.BlockSpec((ts, gw), rows(P_KB // gw)),
            pl.BlockSpec((ts, gw), rows(P_VB // gw)),
            pl.BlockSpec((ts, gw), rows(P_Z // gw)),
            pl.BlockSpec((ts, LANES), rows(P_AB // LANES)),
            pl.BlockSpec((GDN_CONV, 3 * gw), lambda b, j: (0, 0)),
            pl.BlockSpec((1, LANES), lambda b, j: (0, 0)),
            pl.BlockSpec((1, LANES), lambda b, j: (0, 0)),
            pl.BlockSpec((1, GDN_HEAD_DIM), lambda b, j: (0, 0)),
        ],
        out_specs=pl.BlockSpec((ts, gw), rows(0)),
        scratch_shapes=[
            pltpu.VMEM((GDN_HEADS, GDN_HEAD_DIM, GDN_HEAD_DIM), F32),
            pltpu.VMEM((CONV_PAD, 3 * gw), F32),
            pltpu.VMEM((ts + CONV_PAD, 3 * gw), F32),
            pltpu.VMEM((ts, 3 * gw), F32),
            pltpu.VMEM((ts, LANES), F32),
            pltpu.VMEM((ts, LANES), F32),
        ],
        compiler_params=pltpu.CompilerParams(dimension_semantics=("arbitrary", "arbitrary")),
        name="gdn",
    )(p, p, p, p, p, conv_w, alog_row, dtb_row, norm_w)


def _merge_kernel(x_ref, ga_ref, gb_ref, gc_ref, ya_ref, yb_ref, yc_ref, wa_ref, wb_ref, wc_ref, wo_ref,
                  h_ref):
    merged = jax.nn.sigmoid(ga_ref[...]) * _dot(ya_ref[...], wa_ref[...])
    merged += jax.nn.sigmoid(gb_ref[...]) * _dot(yb_ref[...], wb_ref[...])
    merged += jax.nn.sigmoid(gc_ref[...]) * _dot(yc_ref[...], wc_ref[...])
    h_ref[...] = x_ref[...] + _dot(merged.astype(BF16), wo_ref[...])


def _merge_out(x, p, ya, yb, yc, wa, wb, wc, wo, *, tm=256):
    t, d = x.shape

    def resident(shape):
        return pl.BlockSpec(shape, lambda i: (0, 0), pipeline_mode=pl.Buffered(1))

    return pl.pallas_call(
        _merge_kernel,
        out_shape=jax.ShapeDtypeStruct((t, d), F32),
        grid=(t // tm,),
        in_specs=[
            pl.BlockSpec((tm, d), lambda i: (i, 0)),
            pl.BlockSpec((tm, d), lambda i: (i, P_GATE // d)),
            pl.BlockSpec((tm, d), lambda i: (i, P_GATE // d + 1)),
            pl.BlockSpec((tm, d), lambda i: (i, P_GATE // d + 2)),
            pl.BlockSpec((tm, SWA_Q_W), lambda i: (i, 0)),
            pl.BlockSpec((tm, GDN_W), lambda i: (i, 0)),
            pl.BlockSpec((tm, XA_W), lambda i: (i, 0)),
            resident((SWA_Q_W, d)),
            resident((GDN_W, d)),
            resident((XA_W, d)),
            resident((d, d)),
        ],
        out_specs=pl.BlockSpec((tm, d), lambda i: (i, 0)),
        compiler_params=pltpu.CompilerParams(
            dimension_semantics=("arbitrary",), vmem_limit_bytes=VMEM_LIMIT),
        name="merge_out",
    )(x, p, p, p, ya, yb, yc, wa, wb, wc, wo)


def _mlp_kernel(h_ref, g_ref, w1_ref, w2_ref, gf_ref, o_ref, n_ref, acc_ref, *, final_norm):
    j = pl.program_id(1)

    @pl.when(j == 0)
    def _():
        h = h_ref[...]
        n_ref[...] = _rms(h, g_ref[...]).astype(BF16)
        acc_ref[...] = h

    u = _dot(n_ref[...], w1_ref[...])
    acc_ref[...] += _dot(jnp.square(jnp.maximum(u, 0.0)).astype(BF16), w2_ref[...])

    @pl.when(j == pl.num_programs(1) - 1)
    def _():
        h = acc_ref[...]
        o_ref[...] = _rms(h, gf_ref[...]) if final_norm else h


def _mlp(h, g, w1, w2, g_final, *, final_norm, tm=512, tf=512):
    t, d = h.shape
    f = w1.shape[1]
    return pl.pallas_call(
        functools.partial(_mlp_kernel, final_norm=final_norm),
        out_shape=jax.ShapeDtypeStruct((t, d), F32),
        grid=(t // tm, f // tf),
        in_specs=[
            pl.BlockSpec((tm, d), lambda i, j: (i, 0)),
            pl.BlockSpec((1, d), lambda i, j: (0, 0)),
            pl.BlockSpec((d, tf), lambda i, j: (0, j)),
            pl.BlockSpec((tf, d), lambda i, j: (j, 0)),
            pl.BlockSpec((1, d), lambda i, j: (0, 0)),
        ],
        out_specs=pl.BlockSpec((tm, d), lambda i, j: (i, 0)),
        scratch_shapes=[pltpu.VMEM((tm, d), BF16), pltpu.VMEM((tm, d), F32)],
        compiler_params=pltpu.CompilerParams(
            dimension_semantics=("arbitrary", "arbitrary"), vmem_limit_bytes=VMEM_LIMIT),
        name="mlp",
    )(h, g, w1, w2, g_final)


def _reorder_w_in(w):
    d = w.shape[0]
    o = 0
    seg = {}
    for name, width in (("qa", SWA_Q_W), ("ka", SWA_KV_W), ("va", SWA_KV_W), ("qkvb", 3 * GDN_W),
                        ("ab", 2 * GDN_HEADS), ("z", GDN_W), ("qc", XA_W), ("gate", 3 * D_MODEL)):
        seg[name] = w[:, o:o + width]
        o += width
    pad = jnp.zeros((d, P_WIDTH - P_AB - 2 * GDN_HEADS), w.dtype)
    out = jnp.concatenate([seg["gate"], seg["qa"], seg["qkvb"], seg["z"], seg["qc"], seg["ka"], seg["va"],
                           seg["ab"], pad], axis=1)
    return out.astype(BF16)


def _lane_row(v):
    return jnp.zeros((1, LANES), F32).at[0, :v.shape[0]].set(v.astype(F32))


def kernel(x, mem, g_mix, w_in, sinks, conv_w, a_log, dt_bias, gdn_norm_w, g_mem, w_mem_kv, w_swa_up,
           w_gdn_up, w_xa_up, w_out, g_mlp, w_mlp_in, w_mlp_out, g_final):
    batch, seq, d = x.shape
    depth = w_in.shape[0]
    h = x.reshape(batch * seq, d)
    mem2 = mem.reshape(batch * N_MEM, d)
    for l in range(depth):
        p = _in_proj(h, g_mix[l][None], _reorder_w_in(w_in[l]))
        mkv = _mem_kv(mem2, g_mem[l][None], w_mem_kv[l].astype(BF16))
        ya = _swa(p, sinks[l], seq)
        yb = _gdn(p, conv_w[l], _lane_row(a_log[l]), _lane_row(dt_bias[l]), gdn_norm_w[l][None], batch, seq)
        yc = _xattn(p, mkv, seq)
        h = _merge_out(h, p, ya, yb, yc, w_swa_up[l].astype(BF16), w_gdn_up[l].astype(BF16),
                       w_xa_up[l].astype(BF16), w_out[l].astype(BF16))
        h = _mlp(h, g_mlp[l][None], w_mlp_in[l].astype(BF16), w_mlp_out[l].astype(BF16), g_final[None],
                 final_norm=(l == depth - 1))
    return h.reshape(batch, seq, d)
```

```python
import functools

import jax
import jax.numpy as jnp
from jax import lax
from jax.experimental import pallas as pl
from jax.experimental.pallas import tpu as pltpu

F32 = jnp.float32
BF16 = jnp.bfloat16

D_MODEL = 2048
SWA_Q_HEADS = 16
SWA_KV_HEADS = 2
SWA_HEAD_DIM = 64
SWA_WINDOW = 128
GDN_HEADS = 4
GDN_HEAD_DIM = 128
GDN_CONV = 4
GDN_CHUNK = 64
N_MEM = 256
XA_HEADS = 4
XA_HEAD_DIM = 128
D_FF = 4 * D_MODEL
RMS_EPS = 1e-6
L2_EPS = 1e-6

SWA_Q_W = SWA_Q_HEADS * SWA_HEAD_DIM
SWA_KV_W = SWA_KV_HEADS * SWA_HEAD_DIM
GDN_W = GDN_HEADS * GDN_HEAD_DIM
XA_W = XA_HEADS * XA_HEAD_DIM

LANES = 128

P_GATE = 0
P_QA = P_GATE + 3 * D_MODEL
P_QB = P_QA + SWA_Q_W
P_KB = P_QB + GDN_W
P_VB = P_KB + GDN_W
P_Z = P_VB + GDN_W
P_QC = P_Z + GDN_W
P_KA = P_QC + XA_W
P_VA = P_KA + SWA_KV_W
P_AB = P_VA + SWA_KV_W
P_END = P_AB + LANES
IN_PROJ_TN = 1024
P_WIDTH = -(-P_END // IN_PROJ_TN) * IN_PROJ_TN

VMEM_LIMIT = 56 * 1024 * 1024


def _rms(x, g):
    return x * lax.rsqrt(jnp.mean(x * x, axis=-1, keepdims=True) + RMS_EPS) * g


def _dot(a, b):
    return jnp.dot(a, b, preferred_element_type=F32)


def _dot_nt(a, b):
    return lax.dot_general(a, b, (((1,), (1,)), ((), ())), preferred_element_type=F32)


def _dot_f32(a, b):
    return jnp.dot(a, b, preferred_element_type=F32, precision=lax.Precision.HIGHEST)


def _in_proj_kernel(x_ref, g_ref, w_ref, o_ref, ab_ref, n_ref, *, ab_tile, ab_off):
    j = pl.program_id(1)

    @pl.when(j == 0)
    def _():
        n_ref[...] = _rms(x_ref[...], g_ref[...]).astype(BF16)

    acc = _dot(n_ref[...], w_ref[...])
    o_ref[...] = acc.astype(o_ref.dtype)

    @pl.when(j == ab_tile)
    def _():
        ab_ref[...] = acc[:, ab_off:ab_off + LANES]


def _in_proj(x, g, w, *, tm=1024, tn=IN_PROJ_TN):
    t, d = x.shape
    n = w.shape[1]
    return pl.pallas_call(
        functools.partial(_in_proj_kernel, ab_tile=P_AB // tn, ab_off=P_AB % tn),
        out_shape=(jax.ShapeDtypeStruct((t, n), BF16), jax.ShapeDtypeStruct((t, LANES), F32)),
        grid=(t // tm, n // tn),
        in_specs=[
            pl.BlockSpec((tm, d), lambda i, j: (i, 0)),
            pl.BlockSpec((1, d), lambda i, j: (0, 0)),
            pl.BlockSpec((d, tn), lambda i, j: (0, j)),
        ],
        out_specs=(pl.BlockSpec((tm, tn), lambda i, j: (i, j)),
                   pl.BlockSpec((tm, LANES), lambda i, j: (i, 0))),
        scratch_shapes=[pltpu.VMEM((tm, d), BF16)],
        compiler_params=pltpu.CompilerParams(
            dimension_semantics=("arbitrary", "arbitrary"), vmem_limit_bytes=VMEM_LIMIT),
        name="in_proj",
    )(x, g, w)


def _mem_kv_kernel(m_ref, g_ref, w_ref, o_ref):
    o_ref[...] = _dot(_rms(m_ref[...], g_ref[...]).astype(BF16), w_ref[...]).astype(o_ref.dtype)


def _mem_kv(mem, g, w):
    t, d = mem.shape
    n = w.shape[1]
    return pl.pallas_call(
        _mem_kv_kernel,
        out_shape=jax.ShapeDtypeStruct((t, n), BF16),
        grid=(t // N_MEM,),
        in_specs=[
            pl.BlockSpec((N_MEM, d), lambda i: (i, 0)),
            pl.BlockSpec((1, d), lambda i: (0, 0)),
            pl.BlockSpec((d, n), lambda i: (0, 0)),
        ],
        out_specs=pl.BlockSpec((N_MEM, n), lambda i: (i, 0)),
        compiler_params=pltpu.CompilerParams(
            dimension_semantics=("arbitrary",), vmem_limit_bytes=VMEM_LIMIT),
        name="mem_kv",
    )(mem, g, w)


def _swa_kernel(sinks_ref, q_ref, kc_ref, vc_ref, kp_ref, vp_ref, o_ref, *, blocks_per_seq):
    w = SWA_WINDOW
    hd = SWA_HEAD_DIM
    first = (pl.program_id(0) % blocks_per_seq) == 0
    k = jnp.concatenate([kp_ref[...], kc_ref[...]], axis=0)
    v = jnp.concatenate([vp_ref[...], vc_ref[...]], axis=0)
    qi = lax.broadcasted_iota(jnp.int32, (w, 2 * w), 0)
    kj = lax.broadcasted_iota(jnp.int32, (w, 2 * w), 1)
    kj_min = jnp.where(first, w, 0)
    valid = (kj > qi) & (kj <= qi + w) & (kj >= kj_min)
    group = SWA_Q_HEADS // SWA_KV_HEADS
    for h in range(SWA_Q_HEADS):
        hk = h // group
        kh = k[:, hk * hd:(hk + 1) * hd]
        vh = v[:, hk * hd:(hk + 1) * hd]
        qh = q_ref[:, h * hd:(h + 1) * hd]
        s = _dot_nt(qh, kh) * (hd ** -0.5)
        s = jnp.where(valid, s, -jnp.inf)
        sk = sinks_ref[h]
        m = jnp.maximum(jnp.max(s, axis=-1, keepdims=True), sk)
        p = jnp.exp(s - m)
        denom = jnp.sum(p, axis=-1, keepdims=True) + jnp.exp(sk - m)
        pr = (p / denom).astype(BF16)
        o_ref[:, h * hd:(h + 1) * hd] = _dot(pr, vh).astype(o_ref.dtype)


def _swa(p, sinks, seq):
    t = p.shape[0]
    w = SWA_WINDOW
    blocks_per_seq = seq // w
    q_blk = P_QA // SWA_Q_W
    k_blk = P_KA // SWA_KV_W
    v_blk = P_VA // SWA_KV_W

    def prev(i):
        return jnp.maximum(i - 1, 0)

    return pl.pallas_call(
        functools.partial(_swa_kernel, blocks_per_seq=blocks_per_seq),
        out_shape=jax.ShapeDtypeStruct((t, SWA_Q_W), BF16),
        grid=(t // w,),
        in_specs=[
            pl.BlockSpec(memory_space=pltpu.SMEM),
            pl.BlockSpec((w, SWA_Q_W), lambda i: (i, q_blk)),
            pl.BlockSpec((w, SWA_KV_W), lambda i: (i, k_blk)),
            pl.BlockSpec((w, SWA_KV_W), lambda i: (i, v_blk)),
            pl.BlockSpec((w, SWA_KV_W), lambda i: (prev(i), k_blk)),
            pl.BlockSpec((w, SWA_KV_W), lambda i: (prev(i), v_blk)),
        ],
        out_specs=pl.BlockSpec((w, SWA_Q_W), lambda i: (i, 0)),
        compiler_params=pltpu.CompilerParams(dimension_semantics=("arbitrary",)),
        name="swa",
    )(sinks, p, p, p, p, p)


def _xattn_kernel(q_ref, mk_ref, mv_ref, o_ref):
    hd = XA_HEAD_DIM
    for h in range(XA_HEADS):
        sl = slice(h * hd, (h + 1) * hd)
        s = _dot_nt(q_ref[:, sl], mk_ref[:, sl]) * (hd ** -0.5)
        m = jnp.max(s, axis=-1, keepdims=True)
        e = jnp.exp(s - m)
        pr = (e / jnp.sum(e, axis=-1, keepdims=True)).astype(BF16)
        o_ref[:, sl] = _dot(pr, mv_ref[:, sl]).astype(o_ref.dtype)


def _xattn(p, mkv, seq, *, tq=512):
    t = p.shape[0]
    tiles_per_seq = seq // tq
    q_blk = P_QC // XA_W
    return pl.pallas_call(
        _xattn_kernel,
        out_shape=jax.ShapeDtypeStruct((t, XA_W), BF16),
        grid=(t // tq,),
        in_specs=[
            pl.BlockSpec((tq, XA_W), lambda i: (i, q_blk)),
            pl.BlockSpec((N_MEM, XA_W), lambda i: (i // tiles_per_seq, 0)),
            pl.BlockSpec((N_MEM, XA_W), lambda i: (i // tiles_per_seq, 1)),
        ],
        out_specs=pl.BlockSpec((tq, XA_W), lambda i: (i, 0)),
        compiler_params=pltpu.CompilerParams(dimension_semantics=("arbitrary",)),
        name="xattn",
    )(p, mkv, mkv)


GDN_TS = 256
CONV_PAD = 8


def _gdn_kernel(qb_ref, kb_ref, vb_ref, z_ref, ab_ref, cw_ref, alog_ref, dtb_ref, nw_ref, y_ref,
                state_ref, carry_ref, xpad_ref, cv_ref):
    ts = GDN_TS
    c = GDN_CHUNK
    dh = GDN_HEAD_DIM
    gw = GDN_W
    heads = range(GDN_HEADS)
    chunks = [slice(i * c, (i + 1) * c) for i in range(ts // c)]

    @pl.when(pl.program_id(1) == 0)
    def _():
        state_ref[...] = jnp.zeros_like(state_ref)
        carry_ref[...] = jnp.zeros_like(carry_ref)

    xpad_ref[0:CONV_PAD, :] = carry_ref[...]
    xpad_ref[CONV_PAD:, 0:gw] = qb_ref[...].astype(F32)
    xpad_ref[CONV_PAD:, gw:2 * gw] = kb_ref[...].astype(F32)
    xpad_ref[CONV_PAD:, 2 * gw:3 * gw] = vb_ref[...].astype(F32)
    carry_ref[...] = xpad_ref[ts:ts + CONV_PAD, :]
    for cb in range(3 * gw // LANES):
        cs = slice(cb * LANES, (cb + 1) * LANES)
        acc = None
        for i in range(GDN_CONV):
            off = CONV_PAD - (GDN_CONV - 1) + i
            term = cw_ref[i:i + 1, cs] * xpad_ref[off:off + ts, cs]
            acc = term if acc is None else acc + term
        cv_ref[:, cs] = acc * jax.nn.sigmoid(acc)

    ab = ab_ref[...]
    g = -jnp.exp(alog_ref[...]) * jax.nn.softplus(ab + dtb_ref[...])
    beta_all = jax.nn.sigmoid(ab)
    ri = lax.broadcasted_iota(jnp.int32, (ts, ts), 0)
    ci = lax.broadcasted_iota(jnp.int32, (ts, ts), 1)
    same_chunk = (ri // c) == (ci // c)
    causal = same_chunk & (ri >= ci)
    strict = same_chunk & (ri > ci)
    gcum = _dot_f32(causal.astype(F32), g)
    gcum_t = gcum.T
    g_last = jnp.concatenate(
        [jnp.broadcast_to(gcum[rc.stop - 1:rc.stop], (c, LANES)) for rc in chunks], axis=0)
    k_scale = jnp.exp(g_last - gcum)
    eg_all = jnp.exp(gcum)

    q, k, v, gc, beta = [], [], [], [], []
    for h in heads:
        qh = cv_ref[:, h * dh:(h + 1) * dh]
        kh = cv_ref[:, gw + h * dh:gw + (h + 1) * dh]
        q.append(qh * lax.rsqrt(jnp.sum(qh * qh, axis=-1, keepdims=True) + L2_EPS) * (dh ** -0.5))
        k.append(kh * lax.rsqrt(jnp.sum(kh * kh, axis=-1, keepdims=True) + L2_EPS))
        v.append(cv_ref[:, 2 * gw + h * dh:2 * gw + (h + 1) * dh])
        gc.append(gcum[:, h:h + 1])
        beta.append(beta_all[:, GDN_HEADS + h:GDN_HEADS + h + 1])
    qkk = [_dot_nt(jnp.concatenate([q[h], k[h]], axis=0).astype(BF16), k[h].astype(BF16)) for h in heads]
    decay = [jnp.exp(jnp.where(causal, gc[h] - gcum_t[h:h + 1, :], -jnp.inf)) for h in heads]
    qk = [qkk[h][:ts] * decay[h] for h in heads]
    m = [-jnp.where(strict, beta[h] * qkk[h][ts:] * decay[h], 0.0) for h in heads]
    sol = [jnp.concatenate([v[h] * beta[h], k[h] * (beta[h] * eg_all[:, h:h + 1])], axis=-1) for h in heads]
    n_rounds = c.bit_length() - 1
    for r in range(n_rounds):
        last = r + 1 == n_rounds
        mb = [m[h].astype(BF16) for h in heads]
        rhs = [sol[h].astype(BF16) if last else jnp.concatenate([sol[h], m[h]], axis=-1).astype(BF16)
               for h in heads]
        prod = [_dot(mb[h], rhs[h]) for h in heads]
        sol = [sol[h] + prod[h][:, :2 * dh] for h in heads]
        if not last:
            m = [prod[h][:, 2 * dh:] for h in heads]
    k_dec = [k[h] * k_scale[:, h:h + 1] for h in heads]
    q_dec = [q[h] * eg_all[:, h:h + 1] for h in heads]

    x = {}
    for ic, rc in enumerate(chunks):
        for h in heads:
            lhs = jnp.concatenate([k_dec[h][rc].T, qk[h][rc, rc]], axis=0).astype(BF16)
            x[ic, h] = _dot(lhs, sol[h][rc].astype(BF16))

    for ic, rc in enumerate(chunks):
        for h in heads:
            xs = x[ic, h]
            lhs = jnp.concatenate([-xs[:dh, dh:], q_dec[h][rc] - xs[dh:, dh:]], axis=0).astype(BF16)
            s = state_ref[h]
            y = _dot(lhs, s.astype(BF16))
            state_ref[h] = s * eg_all[rc.stop - 1:rc.stop, h:h + 1] + y[:dh] + xs[:dh, :dh]
            o = _rms(y[dh:] + xs[dh:, :dh], nw_ref[...])
            hs = slice(h * dh, (h + 1) * dh)
            z = z_ref[rc, hs].astype(F32)
            y_ref[rc, hs] = (o * (z * jax.nn.sigmoid(z))).astype(y_ref.dtype)


def _gdn(p, ab, conv_w, alog_row, dtb_row, norm_w, batch, seq):
    t = p.shape[0]
    ts = GDN_TS
    nj = seq // ts
    gw = GDN_W

    def rows(col_blk):
        return lambda b, j: (b * nj + j, col_blk)

    return pl.pallas_call(
        _gdn_kernel,
        out_shape=jax.ShapeDtypeStruct((t, gw), BF16),
        grid=(batch, nj),
        in_specs=[
            pl.BlockSpec((ts, gw), rows(P_QB // gw)),
            pl.BlockSpec((ts, gw), rows(P_KB // gw)),
            pl.BlockSpec((ts, gw), rows(P_VB // gw)),
            pl.BlockSpec((ts, gw), rows(P_Z // gw)),
            pl.BlockSpec((ts, LANES), rows(0)),
            pl.BlockSpec((GDN_CONV, 3 * gw), lambda b, j: (0, 0)),
            pl.BlockSpec((1, LANES), lambda b, j: (0, 0)),
            pl.BlockSpec((1, LANES), lambda b, j: (0, 0)),
            pl.BlockSpec((1, GDN_HEAD_DIM), lambda b, j: (0, 0)),
        ],
        out_specs=pl.BlockSpec((ts, gw), rows(0)),
        scratch_shapes=[
            pltpu.VMEM((GDN_HEADS, GDN_HEAD_DIM, GDN_HEAD_DIM), F32),
            pltpu.VMEM((CONV_PAD, 3 * gw), F32),
            pltpu.VMEM((ts + CONV_PAD, 3 * gw), F32),
            pltpu.VMEM((ts, 3 * gw), F32),
        ],
        compiler_params=pltpu.CompilerParams(dimension_semantics=("arbitrary", "arbitrary")),
        name="gdn",
    )(p, p, p, p, ab, conv_w, alog_row, dtb_row, norm_w)


def _merge_kernel(x_ref, ga_ref, gb_ref, gc_ref, ya_ref, yb_ref, yc_ref, wa_ref, wb_ref, wc_ref, wo_ref,
                  h_ref):
    merged = jax.nn.sigmoid(ga_ref[...].astype(F32)) * _dot(ya_ref[...], wa_ref[...])
    merged += jax.nn.sigmoid(gb_ref[...].astype(F32)) * _dot(yb_ref[...], wb_ref[...])
    merged += jax.nn.sigmoid(gc_ref[...].astype(F32)) * _dot(yc_ref[...], wc_ref[...])
    h_ref[...] = x_ref[...] + _dot(merged.astype(BF16), wo_ref[...])


def _merge_out(x, p, ya, yb, yc, wa, wb, wc, wo, *, tm=256):
    t, d = x.shape

    def resident(shape):
        return pl.BlockSpec(shape, lambda i: (0, 0), pipeline_mode=pl.Buffered(1))

    return pl.pallas_call(
        _merge_kernel,
        out_shape=jax.ShapeDtypeStruct((t, d), F32),
        grid=(t // tm,),
        in_specs=[
            pl.BlockSpec((tm, d), lambda i: (i, 0)),
            pl.BlockSpec((tm, d), lambda i: (i, P_GATE // d)),
            pl.BlockSpec((tm, d), lambda i: (i, P_GATE // d + 1)),
            pl.BlockSpec((tm, d), lambda i: (i, P_GATE // d + 2)),
            pl.BlockSpec((tm, SWA_Q_W), lambda i: (i, 0)),
            pl.BlockSpec((tm, GDN_W), lambda i: (i, 0)),
            pl.BlockSpec((tm, XA_W), lambda i: (i, 0)),
            resident((SWA_Q_W, d)),
            resident((GDN_W, d)),
            resident((XA_W, d)),
            resident((d, d)),
        ],
        out_specs=pl.BlockSpec((tm, d), lambda i: (i, 0)),
        compiler_params=pltpu.CompilerParams(
            dimension_semantics=("arbitrary",), vmem_limit_bytes=VMEM_LIMIT),
        name="merge_out",
    )(x, p, p, p, ya, yb, yc, wa, wb, wc, wo)


def _mlp_kernel(h_ref, g_ref, w1_ref, w2_ref, gf_ref, o_ref, n_ref, acc_ref, *, final_norm):
    j = pl.program_id(1)

    @pl.when(j == 0)
    def _():
        h = h_ref[...]
        n_ref[...] = _rms(h, g_ref[...]).astype(BF16)
        acc_ref[...] = h

    u = _dot(n_ref[...], w1_ref[...])
    acc_ref[...] += _dot(jnp.square(jnp.maximum(u, 0.0)).astype(BF16), w2_ref[...])

    @pl.when(j == pl.num_programs(1) - 1)
    def _():
        h = acc_ref[...]
        o_ref[...] = _rms(h, gf_ref[...]) if final_norm else h


def _mlp(h, g, w1, w2, g_final, *, final_norm, tm=512, tf=512):
    t, d = h.shape
    f = w1.shape[1]
    return pl.pallas_call(
        functools.partial(_mlp_kernel, final_norm=final_norm),
        out_shape=jax.ShapeDtypeStruct((t, d), F32),
        grid=(t // tm, f // tf),
        in_specs=[
            pl.BlockSpec((tm, d), lambda i, j: (i, 0)),
            pl.BlockSpec((1, d), lambda i, j: (0, 0)),
            pl.BlockSpec((d, tf), lambda i, j: (0, j)),
            pl.BlockSpec((tf, d), lambda i, j: (j, 0)),
            pl.BlockSpec((1, d), lambda i, j: (0, 0)),
        ],
        out_specs=pl.BlockSpec((tm, d), lambda i, j: (i, 0)),
        scratch_shapes=[pltpu.VMEM((tm, d), BF16), pltpu.VMEM((tm, d), F32)],
        compiler_params=pltpu.CompilerParams(
            dimension_semantics=("arbitrary", "arbitrary"), vmem_limit_bytes=VMEM_LIMIT),
        name="mlp",
    )(h, g, w1, w2, g_final)


def _reorder_w_in(w):
    d = w.shape[0]
    o = 0
    seg = {}
    for name, width in (("qa", SWA_Q_W), ("ka", SWA_KV_W), ("va", SWA_KV_W), ("qkvb", 3 * GDN_W),
                        ("ab", 2 * GDN_HEADS), ("z", GDN_W), ("qc", XA_W), ("gate", 3 * D_MODEL)):
        seg[name] = w[:, o:o + width]
        o += width
    pad = jnp.zeros((d, P_WIDTH - P_AB - 2 * GDN_HEADS), w.dtype)
    out = jnp.concatenate([seg["gate"], seg["qa"], seg["qkvb"], seg["z"], seg["qc"], seg["ka"], seg["va"],
                           seg["ab"], pad], axis=1)
    return out.astype(BF16)


def _lane_row(v):
    return jnp.zeros((1, LANES), F32).at[0, :v.shape[0]].set(v.astype(F32))


def kernel(x, mem, g_mix, w_in, sinks, conv_w, a_log, dt_bias, gdn_norm_w, g_mem, w_mem_kv, w_swa_up,
           w_gdn_up, w_xa_up, w_out, g_mlp, w_mlp_in, w_mlp_out, g_final):
    batch, seq, d = x.shape
    depth = w_in.shape[0]
    h = x.reshape(batch * seq, d)
    mem2 = mem.reshape(batch * N_MEM, d)
    for l in range(depth):
        p, ab = _in_proj(h, g_mix[l][None], _reorder_w_in(w_in[l]))
        mkv = _mem_kv(mem2, g_mem[l][None], w_mem_kv[l].astype(BF16))
        ya = _swa(p, sinks[l], seq)
        yb = _gdn(p, ab, conv_w[l], _lane_row(a_log[l]), _lane_row(dt_bias[l]), gdn_norm_w[l][None],
                  batch, seq)
        yc = _xattn(p, mkv, seq)
        h = _merge_out(h, p, ya, yb, yc, w_swa_up[l].astype(BF16), w_gdn_up[l].astype(BF16),
                       w_xa_up[l].astype(BF16), w_out[l].astype(BF16))
        h = _mlp(h, g_mlp[l][None], w_mlp_in[l].astype(BF16), w_mlp_out[l].astype(BF16), g_final[None],
                 final_norm=(l == depth - 1))
    return h.reshape(batch, seq, d)
```

```python
import functools

import jax
import jax.numpy as jnp
from jax import lax
from jax.experimental import pallas as pl
from jax.experimental.pallas import tpu as pltpu

F32 = jnp.float32
BF16 = jnp.bfloat16

D_MODEL = 2048
SWA_Q_HEADS = 16
SWA_KV_HEADS = 2
SWA_HEAD_DIM = 64
SWA_WINDOW = 128
GDN_HEADS = 4
GDN_HEAD_DIM = 128
GDN_CONV = 4
GDN_CHUNK = 64
N_MEM = 256
XA_HEADS = 4
XA_HEAD_DIM = 128
D_FF = 4 * D_MODEL
RMS_EPS = 1e-6
L2_EPS = 1e-6

SWA_Q_W = SWA_Q_HEADS * SWA_HEAD_DIM
SWA_KV_W = SWA_KV_HEADS * SWA_HEAD_DIM
GDN_W = GDN_HEADS * GDN_HEAD_DIM
XA_W = XA_HEADS * XA_HEAD_DIM

LANES = 128

P_GATE = 0
P_QA = P_GATE + 3 * D_MODEL
P_QB = P_QA + SWA_Q_W
P_KB = P_QB + GDN_W
P_VB = P_KB + GDN_W
P_Z = P_VB + GDN_W
P_QC = P_Z + GDN_W
P_KA = P_QC + XA_W
P_VA = P_KA + SWA_KV_W
P_AB = P_VA + SWA_KV_W
P_END = P_AB + LANES
IN_PROJ_TN = 1024
P_WIDTH = -(-P_END // IN_PROJ_TN) * IN_PROJ_TN

VMEM_LIMIT = 56 * 1024 * 1024


def _rms(x, g):
    return x * lax.rsqrt(jnp.mean(x * x, axis=-1, keepdims=True) + RMS_EPS) * g


def _dot(a, b):
    return jnp.dot(a, b, preferred_element_type=F32)


def _dot_nt(a, b):
    return lax.dot_general(a, b, (((1,), (1,)), ((), ())), preferred_element_type=F32)


def _dot_f32(a, b):
    return jnp.dot(a, b, preferred_element_type=F32, precision=lax.Precision.HIGHEST)


def _in_proj_kernel(x_ref, g_ref, w_ref, o_ref, ab_ref, n_ref, *, ab_tile, ab_off):
    j = pl.program_id(1)

    @pl.when(j == 0)
    def _():
        n_ref[...] = _rms(x_ref[...], g_ref[...]).astype(BF16)

    acc = _dot(n_ref[...], w_ref[...])
    o_ref[...] = acc.astype(o_ref.dtype)

    @pl.when(j == ab_tile)
    def _():
        ab_ref[...] = acc[:, ab_off:ab_off + LANES]


def _in_proj(x, g, w, *, tm=1024, tn=IN_PROJ_TN):
    t, d = x.shape
    n = w.shape[1]
    return pl.pallas_call(
        functools.partial(_in_proj_kernel, ab_tile=P_AB // tn, ab_off=P_AB % tn),
        out_shape=(jax.ShapeDtypeStruct((t, n), BF16), jax.ShapeDtypeStruct((t, LANES), F32)),
        grid=(t // tm, n // tn),
        in_specs=[
            pl.BlockSpec((tm, d), lambda i, j: (i, 0)),
            pl.BlockSpec((1, d), lambda i, j: (0, 0)),
            pl.BlockSpec((d, tn), lambda i, j: (0, j)),
        ],
        out_specs=(pl.BlockSpec((tm, tn), lambda i, j: (i, j)),
                   pl.BlockSpec((tm, LANES), lambda i, j: (i, 0))),
        scratch_shapes=[pltpu.VMEM((tm, d), BF16)],
        compiler_params=pltpu.CompilerParams(
            dimension_semantics=("arbitrary", "arbitrary"), vmem_limit_bytes=VMEM_LIMIT),
        name="in_proj",
    )(x, g, w)


def _mem_kv_kernel(m_ref, g_ref, w_ref, o_ref):
    o_ref[...] = _dot(_rms(m_ref[...], g_ref[...]).astype(BF16), w_ref[...]).astype(o_ref.dtype)


def _mem_kv(mem, g, w):
    t, d = mem.shape
    n = w.shape[1]
    return pl.pallas_call(
        _mem_kv_kernel,
        out_shape=jax.ShapeDtypeStruct((t, n), BF16),
        grid=(t // N_MEM,),
        in_specs=[
            pl.BlockSpec((N_MEM, d), lambda i: (i, 0)),
            pl.BlockSpec((1, d), lambda i: (0, 0)),
            pl.BlockSpec((d, n), lambda i: (0, 0)),
        ],
        out_specs=pl.BlockSpec((N_MEM, n), lambda i: (i, 0)),
        compiler_params=pltpu.CompilerParams(
            dimension_semantics=("arbitrary",), vmem_limit_bytes=VMEM_LIMIT),
        name="mem_kv",
    )(mem, g, w)


def _swa_kernel(sinks_ref, q_ref, kc_ref, vc_ref, kp_ref, vp_ref, o_ref, *, blocks_per_seq):
    w = SWA_WINDOW
    hd = SWA_HEAD_DIM
    first = (pl.program_id(0) % blocks_per_seq) == 0
    group = SWA_Q_HEADS // SWA_KV_HEADS
    k = (jnp.concatenate([kp_ref[...], kc_ref[...]], axis=0).astype(F32) * (hd ** -0.5)).astype(BF16)
    v = jnp.concatenate([vp_ref[...], vc_ref[...]], axis=0)
    ones = jnp.ones((2 * w, hd), BF16)
    qi = lax.broadcasted_iota(jnp.int32, (w, 2 * w), 0)
    kj = lax.broadcasted_iota(jnp.int32, (w, 2 * w), 1)
    kj_min = jnp.where(first, w, 0)
    valid = (kj > qi) & (kj <= qi + w) & (kj >= kj_min)
    heads = range(SWA_Q_HEADS)
    kh = [k[:, hk * hd:(hk + 1) * hd] for hk in range(SWA_KV_HEADS)]
    vh = [jnp.concatenate([v[:, hk * hd:(hk + 1) * hd], ones], axis=1) for hk in range(SWA_KV_HEADS)]
    s = [_dot_nt(q_ref[:, h * hd:(h + 1) * hd], kh[h // group]) for h in heads]
    s = [jnp.where(valid, s[h], -jnp.inf) for h in heads]
    m = [jnp.maximum(jnp.max(s[h], axis=-1, keepdims=True), sinks_ref[h]) for h in heads]
    p = [jnp.exp(s[h] - m[h]).astype(BF16) for h in heads]
    ov = [_dot(p[h], vh[h // group]) for h in heads]
    for h in heads:
        o = ov[h][:, :hd] / (ov[h][:, hd:hd + 1] + jnp.exp(sinks_ref[h] - m[h]))
        o_ref[:, h * hd:(h + 1) * hd] = o.astype(o_ref.dtype)


def _swa(p, sinks, seq):
    t = p.shape[0]
    w = SWA_WINDOW
    blocks_per_seq = seq // w
    q_blk = P_QA // SWA_Q_W
    k_blk = P_KA // SWA_KV_W
    v_blk = P_VA // SWA_KV_W

    def prev(i):
        return jnp.maximum(i - 1, 0)

    return pl.pallas_call(
        functools.partial(_swa_kernel, blocks_per_seq=blocks_per_seq),
        out_shape=jax.ShapeDtypeStruct((t, SWA_Q_W), BF16),
        grid=(t // w,),
        in_specs=[
            pl.BlockSpec(memory_space=pltpu.SMEM),
            pl.BlockSpec((w, SWA_Q_W), lambda i: (i, q_blk)),
            pl.BlockSpec((w, SWA_KV_W), lambda i: (i, k_blk)),
            pl.BlockSpec((w, SWA_KV_W), lambda i: (i, v_blk)),
            pl.BlockSpec((w, SWA_KV_W), lambda i: (prev(i), k_blk)),
            pl.BlockSpec((w, SWA_KV_W), lambda i: (prev(i), v_blk)),
        ],
        out_specs=pl.BlockSpec((w, SWA_Q_W), lambda i: (i, 0)),
        compiler_params=pltpu.CompilerParams(dimension_semantics=("arbitrary",)),
        name="swa",
    )(sinks, p, p, p, p, p)


def _xattn_kernel(q_ref, mk_ref, mv_ref, o_ref):
    hd = XA_HEAD_DIM
    ones = jnp.ones((N_MEM, hd), BF16)
    sub = 128
    units = [(slice(r, r + sub), slice(h * hd, (h + 1) * hd))
             for r in range(0, q_ref.shape[0], sub) for h in range(XA_HEADS)]
    mv = [jnp.concatenate([mv_ref[:, h * hd:(h + 1) * hd], ones], axis=1) for h in range(XA_HEADS)]
    s = [_dot_nt(q_ref[rs, sl], mk_ref[:, sl]) * (hd ** -0.5) for rs, sl in units]
    e = [jnp.exp(si - jnp.max(si, axis=-1, keepdims=True)).astype(BF16) for si in s]
    oe = [_dot(ei, mv[i % XA_HEADS]) for i, ei in enumerate(e)]
    for (rs, sl), oi in zip(units, oe):
        o_ref[rs, sl] = (oi[:, :hd] / oi[:, hd:hd + 1]).astype(o_ref.dtype)


def _xattn(p, mkv, seq, *, tq=512):
    t = p.shape[0]
    tiles_per_seq = seq // tq
    q_blk = P_QC // XA_W
    return pl.pallas_call(
        _xattn_kernel,
        out_shape=jax.ShapeDtypeStruct((t, XA_W), BF16),
        grid=(t // tq,),
        in_specs=[
            pl.BlockSpec((tq, XA_W), lambda i: (i, q_blk)),
            pl.BlockSpec((N_MEM, XA_W), lambda i: (i // tiles_per_seq, 0)),
            pl.BlockSpec((N_MEM, XA_W), lambda i: (i // tiles_per_seq, 1)),
        ],
        out_specs=pl.BlockSpec((tq, XA_W), lambda i: (i, 0)),
        compiler_params=pltpu.CompilerParams(dimension_semantics=("arbitrary",)),
        name="xattn",
    )(p, mkv, mkv)


GDN_TS = 256
CONV_PAD = 8


def _gdn_kernel(qb_ref, kb_ref, vb_ref, z_ref, ab_ref, cw_ref, alog_ref, dtb_ref, nw_ref, y_ref,
                state_ref, carry_ref, xpad_ref, cv_ref):
    ts = GDN_TS
    c = GDN_CHUNK
    dh = GDN_HEAD_DIM
    gw = GDN_W
    heads = range(GDN_HEADS)
    chunks = [slice(i * c, (i + 1) * c) for i in range(ts // c)]

    @pl.when(pl.program_id(1) == 0)
    def _():
        state_ref[...] = jnp.zeros_like(state_ref)
        carry_ref[...] = jnp.zeros_like(carry_ref)

    xpad_ref[0:CONV_PAD, :] = carry_ref[...]
    xpad_ref[CONV_PAD:, 0:gw] = qb_ref[...].astype(F32)
    xpad_ref[CONV_PAD:, gw:2 * gw] = kb_ref[...].astype(F32)
    xpad_ref[CONV_PAD:, 2 * gw:3 * gw] = vb_ref[...].astype(F32)
    carry_ref[...] = xpad_ref[ts:ts + CONV_PAD, :]
    for cb in range(3 * gw // LANES):
        cs = slice(cb * LANES, (cb + 1) * LANES)
        acc = None
        for i in range(GDN_CONV):
            off = CONV_PAD - (GDN_CONV - 1) + i
            term = cw_ref[i:i + 1, cs] * xpad_ref[off:off + ts, cs]
            acc = term if acc is None else acc + term
        cv_ref[:, cs] = acc * jax.nn.sigmoid(acc)

    ab = ab_ref[...]
    g = -jnp.exp(alog_ref[...]) * jax.nn.softplus(ab + dtb_ref[...])
    beta_all = jax.nn.sigmoid(ab)
    ri = lax.broadcasted_iota(jnp.int32, (ts, ts), 0)
    ci = lax.broadcasted_iota(jnp.int32, (ts, ts), 1)
    same_chunk = (ri // c) == (ci // c)
    causal = same_chunk & (ri >= ci)
    strict = same_chunk & (ri > ci)
    gcum = _dot_f32(causal.astype(F32), g)
    gcum_t = gcum.T
    g_last = jnp.concatenate(
        [jnp.broadcast_to(gcum[rc.stop - 1:rc.stop], (c, LANES)) for rc in chunks], axis=0)
    k_scale = jnp.exp(g_last - gcum)
    eg_all = jnp.exp(gcum)

    q, k, v, gc, beta = [], [], [], [], []
    for h in heads:
        qh = cv_ref[:, h * dh:(h + 1) * dh]
        kh = cv_ref[:, gw + h * dh:gw + (h + 1) * dh]
        q.append(qh * lax.rsqrt(jnp.sum(qh * qh, axis=-1, keepdims=True) + L2_EPS) * (dh ** -0.5))
        k.append(kh * lax.rsqrt(jnp.sum(kh * kh, axis=-1, keepdims=True) + L2_EPS))
        v.append(cv_ref[:, 2 * gw + h * dh:2 * gw + (h + 1) * dh])
        gc.append(gcum[:, h:h + 1])
        beta.append(beta_all[:, GDN_HEADS + h:GDN_HEADS + h + 1])
    qkk = [_dot_nt(jnp.concatenate([q[h], k[h]], axis=0).astype(BF16), k[h].astype(BF16)) for h in heads]
    decay = [jnp.exp(jnp.where(causal, gc[h] - gcum_t[h:h + 1, :], -jnp.inf)) for h in heads]
    qk = [qkk[h][:ts] * decay[h] for h in heads]
    m = [-jnp.where(strict, beta[h] * qkk[h][ts:] * decay[h], 0.0) for h in heads]
    sol = [jnp.concatenate([v[h] * beta[h], k[h] * (beta[h] * eg_all[:, h:h + 1])], axis=-1) for h in heads]
    n_rounds = c.bit_length() - 1
    for r in range(n_rounds):
        last = r + 1 == n_rounds
        mb = [m[h].astype(BF16) for h in heads]
        rhs = [sol[h].astype(BF16) if last else jnp.concatenate([sol[h], m[h]], axis=-1).astype(BF16)
               for h in heads]
        prod = [_dot(mb[h], rhs[h]) for h in heads]
        sol = [sol[h] + prod[h][:, :2 * dh] for h in heads]
        if not last:
            m = [prod[h][:, 2 * dh:] for h in heads]
    k_dec = [k[h] * k_scale[:, h:h + 1] for h in heads]
    q_dec = [q[h] * eg_all[:, h:h + 1] for h in heads]

    x = {}
    for ic, rc in enumerate(chunks):
        for h in heads:
            lhs = jnp.concatenate([k_dec[h][rc].T, qk[h][rc, rc]], axis=0).astype(BF16)
            x[ic, h] = _dot(lhs, sol[h][rc].astype(BF16))

    for ic, rc in enumerate(chunks):
        for h in heads:
            xs = x[ic, h]
            lhs = jnp.concatenate([-xs[:dh, dh:], q_dec[h][rc] - xs[dh:, dh:]], axis=0).astype(BF16)
            s = state_ref[h]
            y = _dot(lhs, s.astype(BF16))
            state_ref[h] = s * eg_all[rc.stop - 1:rc.stop, h:h + 1] + y[:dh] + xs[:dh, :dh]
            o = _rms(y[dh:] + xs[dh:, :dh], nw_ref[...])
            hs = slice(h * dh, (h + 1) * dh)
            z = z_ref[rc, hs].astype(F32)
            y_ref[rc, hs] = (o * (z * jax.nn.sigmoid(z))).astype(y_ref.dtype)


def _gdn(p, ab, conv_w, alog_row, dtb_row, norm_w, batch, seq):
    t = p.shape[0]
    ts = GDN_TS
    nj = seq // ts
    gw = GDN_W

    def rows(col_blk):
        return lambda b, j: (b * nj + j, col_blk)

    return pl.pallas_call(
        _gdn_kernel,
        out_shape=jax.ShapeDtypeStruct((t, gw), BF16),
        grid=(batch, nj),
        in_specs=[
            pl.BlockSpec((ts, gw), rows(P_QB // gw)),
            pl.BlockSpec((ts, gw), rows(P_KB // gw)),
            pl.BlockSpec((ts, gw), rows(P_VB // gw)),
            pl.BlockSpec((ts, gw), rows(P_Z // gw)),
            pl.BlockSpec((ts, LANES), rows(0)),
            pl.BlockSpec((GDN_CONV, 3 * gw), lambda b, j: (0, 0)),
            pl.BlockSpec((1, LANES), lambda b, j: (0, 0)),
            pl.BlockSpec((1, LANES), lambda b, j: (0, 0)),
            pl.BlockSpec((1, GDN_HEAD_DIM), lambda b, j: (0, 0)),
        ],
        out_specs=pl.BlockSpec((ts, gw), rows(0)),
        scratch_shapes=[
            pltpu.VMEM((GDN_HEADS, GDN_HEAD_DIM, GDN_HEAD_DIM), F32),
            pltpu.VMEM((CONV_PAD, 3 * gw), F32),
            pltpu.VMEM((ts + CONV_PAD, 3 * gw), F32),
            pltpu.VMEM((ts, 3 * gw), F32),
        ],
        compiler_params=pltpu.CompilerParams(dimension_semantics=("arbitrary", "arbitrary")),
        name="gdn",
    )(p, p, p, p, ab, conv_w, alog_row, dtb_row, norm_w)


def _merge_kernel(x_ref, ga_ref, gb_ref, gc_ref, ya_ref, yb_ref, yc_ref, wa_ref, wb_ref, wc_ref, wo_ref,
                  h_ref):
    merged = jax.nn.sigmoid(ga_ref[...].astype(F32)) * _dot(ya_ref[...], wa_ref[...])
    merged += jax.nn.sigmoid(gb_ref[...].astype(F32)) * _dot(yb_ref[...], wb_ref[...])
    merged += jax.nn.sigmoid(gc_ref[...].astype(F32)) * _dot(yc_ref[...], wc_ref[...])
    h_ref[...] = x_ref[...] + _dot(merged.astype(BF16), wo_ref[...])


def _merge_out(x, p, ya, yb, yc, wa, wb, wc, wo, *, tm=256):
    t, d = x.shape

    def resident(shape):
        return pl.BlockSpec(shape, lambda i: (0, 0), pipeline_mode=pl.Buffered(1))

    return pl.pallas_call(
        _merge_kernel,
        out_shape=jax.ShapeDtypeStruct((t, d), F32),
        grid=(t // tm,),
        in_specs=[
            pl.BlockSpec((tm, d), lambda i: (i, 0)),
            pl.BlockSpec((tm, d), lambda i: (i, P_GATE // d)),
            pl.BlockSpec((tm, d), lambda i: (i, P_GATE // d + 1)),
            pl.BlockSpec((tm, d), lambda i: (i, P_GATE // d + 2)),
            pl.BlockSpec((tm, SWA_Q_W), lambda i: (i, 0)),
            pl.BlockSpec((tm, GDN_W), lambda i: (i, 0)),
            pl.BlockSpec((tm, XA_W), lambda i: (i, 0)),
            resident((SWA_Q_W, d)),
            resident((GDN_W, d)),
            resident((XA_W, d)),
            resident((d, d)),
        ],
        out_specs=pl.BlockSpec((tm, d), lambda i: (i, 0)),
        compiler_params=pltpu.CompilerParams(
            dimension_semantics=("arbitrary",), vmem_limit_bytes=VMEM_LIMIT),
        name="merge_out",
    )(x, p, p, p, ya, yb, yc, wa, wb, wc, wo)


def _mlp_kernel(h_ref, g_ref, w1_ref, w2_ref, gf_ref, o_ref, n_ref, acc_ref, *, final_norm):
    j = pl.program_id(1)

    @pl.when(j == 0)
    def _():
        h = h_ref[...]
        n_ref[...] = _rms(h, g_ref[...]).astype(BF16)
        acc_ref[...] = h

    u = _dot(n_ref[...], w1_ref[...])
    acc_ref[...] += _dot(jnp.square(jnp.maximum(u, 0.0)).astype(BF16), w2_ref[...])

    @pl.when(j == pl.num_programs(1) - 1)
    def _():
        h = acc_ref[...]
        o_ref[...] = _rms(h, gf_ref[...]) if final_norm else h


def _mlp(h, g, w1, w2, g_final, *, final_norm, tm=512, tf=1024):
    t, d = h.shape
    f = w1.shape[1]
    return pl.pallas_call(
        functools.partial(_mlp_kernel, final_norm=final_norm),
        out_shape=jax.ShapeDtypeStruct((t, d), F32),
        grid=(t // tm, f // tf),
        in_specs=[
            pl.BlockSpec((tm, d), lambda i, j: (i, 0)),
            pl.BlockSpec((1, d), lambda i, j: (0, 0)),
            pl.BlockSpec((d, tf), lambda i, j: (0, j)),
            pl.BlockSpec((tf, d), lambda i, j: (j, 0)),
            pl.BlockSpec((1, d), lambda i, j: (0, 0)),
        ],
        out_specs=pl.BlockSpec((tm, d), lambda i, j: (i, 0)),
        scratch_shapes=[pltpu.VMEM((tm, d), BF16), pltpu.VMEM((tm, d), F32)],
        compiler_params=pltpu.CompilerParams(
            dimension_semantics=("arbitrary", "arbitrary"), vmem_limit_bytes=VMEM_LIMIT),
        name="mlp",
    )(h, g, w1, w2, g_final)


AB_W = 2 * GDN_HEADS
W_PREP_TILE = 256
W_COPY, W_SHIFT, W_AB = 0, 1, 2


def _w_in_tile_table():
    src = {}
    o = 0
    for name, width in (("qa", SWA_Q_W), ("ka", SWA_KV_W), ("va", SWA_KV_W), ("qkvb", 3 * GDN_W),
                        ("ab", AB_W), ("z", GDN_W), ("qc", XA_W), ("gate", 3 * D_MODEL)):
        src[name] = o
        o += width
    segments = (("gate", P_GATE, 3 * D_MODEL), ("qa", P_QA, SWA_Q_W), ("qkvb", P_QB, 3 * GDN_W),
                ("z", P_Z, GDN_W), ("qc", P_QC, XA_W), ("ka", P_KA, SWA_KV_W), ("va", P_VA, SWA_KV_W))
    blk, mode = [], []
    for t in range(P_WIDTH // W_PREP_TILE):
        out_off = t * W_PREP_TILE
        if out_off >= P_AB:
            assert out_off == P_AB and src["ab"] % W_PREP_TILE == 0
            blk.append(src["ab"] // W_PREP_TILE)
            mode.append(W_AB)
            continue
        (name, seg_off, _), = [s for s in segments if s[1] <= out_off < s[1] + s[2]]
        s_off = src[name] + out_off - seg_off
        assert s_off % W_PREP_TILE in (0, AB_W)
        blk.append(s_off // W_PREP_TILE)
        mode.append(W_SHIFT if s_off % W_PREP_TILE else W_COPY)
    return blk, mode


def _w_prep_kernel(blk_ref, mode_ref, a_ref, b_ref, o_ref):
    mode = mode_ref[pl.program_id(0)]

    @pl.when(mode == W_COPY)
    def _():
        o_ref[...] = a_ref[...].astype(BF16)

    @pl.when(mode == W_SHIFT)
    def _():
        o_ref[...] = jnp.concatenate([a_ref[:, AB_W:], b_ref[:, :AB_W]], axis=1).astype(BF16)

    @pl.when(mode == W_AB)
    def _():
        lane = lax.broadcasted_iota(jnp.int32, a_ref.shape, 1)
        o_ref[...] = jnp.where(lane < AB_W, a_ref[...], 0.0).astype(BF16)


def _reorder_w_in(w_in, l):
    d = w_in.shape[1]
    blk, mode = _w_in_tile_table()
    tw = W_PREP_TILE
    return pl.pallas_call(
        _w_prep_kernel,
        out_shape=jax.ShapeDtypeStruct((d, P_WIDTH), BF16),
        grid_spec=pltpu.PrefetchScalarGridSpec(
            num_scalar_prefetch=2,
            grid=(P_WIDTH // tw,),
            in_specs=[
                pl.BlockSpec((None, d, tw), lambda t, blk, mode: (l, 0, blk[t])),
                pl.BlockSpec((None, d, LANES), lambda t, blk, mode: (l, 0, (blk[t] + 1) * (tw // LANES))),
            ],
            out_specs=pl.BlockSpec((d, tw), lambda t, blk, mode: (0, t)),
        ),
        compiler_params=pltpu.CompilerParams(dimension_semantics=("arbitrary",)),
        name="w_prep",
    )(jnp.asarray(blk, jnp.int32), jnp.asarray(mode, jnp.int32), w_in, w_in)


def _lane_row(v):
    return jnp.zeros((1, LANES), F32).at[0, :v.shape[0]].set(v.astype(F32))


def kernel(x, mem, g_mix, w_in, sinks, conv_w, a_log, dt_bias, gdn_norm_w, g_mem, w_mem_kv, w_swa_up,
           w_gdn_up, w_xa_up, w_out, g_mlp, w_mlp_in, w_mlp_out, g_final):
    batch, seq, d = x.shape
    depth = w_in.shape[0]
    h = x.reshape(batch * seq, d)
    mem2 = mem.reshape(batch * N_MEM, d)
    for l in range(depth):
        p, ab = _in_proj(h, g_mix[l][None], _reorder_w_in(w_in, l))
        mkv = _mem_kv(mem2, g_mem[l][None], w_mem_kv[l].astype(BF16))
        ya = _swa(p, sinks[l], seq)
        yb = _gdn(p, ab, conv_w[l], _lane_row(a_log[l]), _lane_row(dt_bias[l]), gdn_norm_w[l][None],
                  batch, seq)
        yc = _xattn(p, mkv, seq)
        h = _merge_out(h, p, ya, yb, yc, w_swa_up[l].astype(BF16), w_gdn_up[l].astype(BF16),
                       w_xa_up[l].astype(BF16), w_out[l].astype(BF16))
        h = _mlp(h, g_mlp[l][None], w_mlp_in[l].astype(BF16), w_mlp_out[l].astype(BF16), g_final[None],
                 final_norm=(l == depth - 1))
    return h.reshape(batch, seq, d)
```

```python
import functools

import jax
import jax.numpy as jnp
from jax import lax
from jax.experimental import pallas as pl
from jax.experimental.pallas import tpu as pltpu

F32 = jnp.float32
BF16 = jnp.bfloat16

D_MODEL = 2048
SWA_Q_HEADS = 16
SWA_KV_HEADS = 2
SWA_HEAD_DIM = 64
SWA_WINDOW = 128
GDN_HEADS = 4
GDN_HEAD_DIM = 128
GDN_CONV = 4
GDN_CHUNK = 64
N_MEM = 256
XA_HEADS = 4
XA_HEAD_DIM = 128
D_FF = 4 * D_MODEL
RMS_EPS = 1e-6
L2_EPS = 1e-6

SWA_Q_W = SWA_Q_HEADS * SWA_HEAD_DIM
SWA_KV_W = SWA_KV_HEADS * SWA_HEAD_DIM
GDN_W = GDN_HEADS * GDN_HEAD_DIM
XA_W = XA_HEADS * XA_HEAD_DIM

LANES = 128

AB_W = 2 * GDN_HEADS

R_QA = 0
R_KA = R_QA + SWA_Q_W
R_VA = R_KA + SWA_KV_W
R_QB = R_VA + SWA_KV_W
R_AB = R_QB + 3 * GDN_W
R_Z = R_AB + AB_W
R_QC = R_Z + GDN_W
R_GATE = R_QC + XA_W
R_END = R_GATE + 3 * D_MODEL

IN_PROJ_TN = 1024
IN_PROJ_TILE_STARTS = (
    tuple(R_GATE + k * IN_PROJ_TN for k in range(3 * D_MODEL // IN_PROJ_TN))
    + tuple(range(0, R_Z, IN_PROJ_TN))
    + tuple(R_Z + k * IN_PROJ_TN for k in range((R_GATE - R_Z) // IN_PROJ_TN)))
P_WIDTH = len(IN_PROJ_TILE_STARTS) * IN_PROJ_TN
P_GATE = 0
P_FRONT = 3 * D_MODEL
P_QA = P_FRONT + R_QA
P_KA = P_FRONT + R_KA
P_VA = P_FRONT + R_VA
P_QB = P_FRONT + R_QB
P_AB = P_FRONT + R_AB
P_Z = P_FRONT + -(-R_Z // IN_PROJ_TN) * IN_PROJ_TN
P_QC = P_Z + GDN_W
assert (R_GATE - R_Z) % IN_PROJ_TN == 0 and (3 * D_MODEL) % IN_PROJ_TN == 0
assert all(s % 8 == 0 for s in IN_PROJ_TILE_STARTS) and P_AB % LANES == 0 and P_QC + XA_W == P_WIDTH

VMEM_LIMIT = 56 * 1024 * 1024


def _rms(x, g):
    return x * lax.rsqrt(jnp.mean(x * x, axis=-1, keepdims=True) + RMS_EPS) * g


def _dot(a, b):
    return jnp.dot(a, b, preferred_element_type=F32)


def _dot_nt(a, b):
    return lax.dot_general(a, b, (((1,), (1,)), ((), ())), preferred_element_type=F32)


def _dot_f32(a, b):
    return jnp.dot(a, b, preferred_element_type=F32, precision=lax.Precision.HIGHEST)


def _in_proj_kernel(starts_ref, x_ref, g_ref, wt_ref, o_ref, ab_ref, n_ref, *, ab_tile, ab_off):
    del starts_ref
    j = pl.program_id(1)

    @pl.when(j == 0)
    def _():
        n_ref[...] = _rms(x_ref[...], g_ref[...]).astype(BF16)

    acc = _dot_nt(n_ref[...], wt_ref[...])
    o_ref[...] = acc.astype(o_ref.dtype)

    @pl.when(j == ab_tile)
    def _():
        ab_ref[...] = acc[:, ab_off:ab_off + LANES]


def _in_proj(x, g, wt, *, tm=1024, tn=IN_PROJ_TN):
    t, d = x.shape
    return pl.pallas_call(
        functools.partial(_in_proj_kernel, ab_tile=P_AB // tn, ab_off=P_AB % tn),
        out_shape=(jax.ShapeDtypeStruct((t, P_WIDTH), BF16), jax.ShapeDtypeStruct((t, LANES), F32)),
        grid_spec=pltpu.PrefetchScalarGridSpec(
            num_scalar_prefetch=1,
            grid=(t // tm, P_WIDTH // tn),
            in_specs=[
                pl.BlockSpec((tm, d), lambda i, j, starts: (i, 0)),
                pl.BlockSpec((1, d), lambda i, j, starts: (0, 0)),
                pl.BlockSpec((pl.Element(tn), pl.Element(d)),
                             lambda i, j, starts: (pl.multiple_of(starts[j], 8), 0)),
            ],
            out_specs=(pl.BlockSpec((tm, tn), lambda i, j, starts: (i, j)),
                       pl.BlockSpec((tm, LANES), lambda i, j, starts: (i, 0))),
            scratch_shapes=[pltpu.VMEM((tm, d), BF16)],
        ),
        compiler_params=pltpu.CompilerParams(
            dimension_semantics=("arbitrary", "arbitrary"), vmem_limit_bytes=VMEM_LIMIT),
        name="in_proj",
    )(jnp.asarray(IN_PROJ_TILE_STARTS, jnp.int32), x, g, wt)


def _mem_kv_kernel(m_ref, g_ref, w_ref, o_ref):
    o_ref[...] = _dot(_rms(m_ref[...], g_ref[...]).astype(BF16), w_ref[...]).astype(o_ref.dtype)


def _mem_kv(mem, g, w):
    t, d = mem.shape
    n = w.shape[1]
    return pl.pallas_call(
        _mem_kv_kernel,
        out_shape=jax.ShapeDtypeStruct((t, n), BF16),
        grid=(t // N_MEM,),
        in_specs=[
            pl.BlockSpec((N_MEM, d), lambda i: (i, 0)),
            pl.BlockSpec((1, d), lambda i: (0, 0)),
            pl.BlockSpec((d, n), lambda i: (0, 0)),
        ],
        out_specs=pl.BlockSpec((N_MEM, n), lambda i: (i, 0)),
        compiler_params=pltpu.CompilerParams(
            dimension_semantics=("arbitrary",), vmem_limit_bytes=VMEM_LIMIT),
        name="mem_kv",
    )(mem, g, w)


def _swa_kernel(sinks_ref, q_ref, kc_ref, vc_ref, kp_ref, vp_ref, o_ref, *, blocks_per_seq):
    w = SWA_WINDOW
    hd = SWA_HEAD_DIM
    first = (pl.program_id(0) % blocks_per_seq) == 0
    group = SWA_Q_HEADS // SWA_KV_HEADS
    k = (jnp.concatenate([kp_ref[...], kc_ref[...]], axis=0).astype(F32) * (hd ** -0.5)).astype(BF16)
    v = jnp.concatenate([vp_ref[...], vc_ref[...]], axis=0)
    ones = jnp.ones((2 * w, hd), BF16)
    qi = lax.broadcasted_iota(jnp.int32, (w, 2 * w), 0)
    kj = lax.broadcasted_iota(jnp.int32, (w, 2 * w), 1)
    kj_min = jnp.where(first, w, 0)
    valid = (kj > qi) & (kj <= qi + w) & (kj >= kj_min)
    heads = range(SWA_Q_HEADS)
    kh = [k[:, hk * hd:(hk + 1) * hd] for hk in range(SWA_KV_HEADS)]
    vh = [jnp.concatenate([v[:, hk * hd:(hk + 1) * hd], ones], axis=1) for hk in range(SWA_KV_HEADS)]
    s = [_dot_nt(q_ref[:, h * hd:(h + 1) * hd], kh[h // group]) for h in heads]
    s = [jnp.where(valid, s[h], -jnp.inf) for h in heads]
    m = [jnp.maximum(jnp.max(s[h], axis=-1, keepdims=True), sinks_ref[h]) for h in heads]
    p = [jnp.exp(s[h] - m[h]).astype(BF16) for h in heads]
    ov = [_dot(p[h], vh[h // group]) for h in heads]
    for h in heads:
        o = ov[h][:, :hd] / (ov[h][:, hd:hd + 1] + jnp.exp(sinks_ref[h] - m[h]))
        o_ref[:, h * hd:(h + 1) * hd] = o.astype(o_ref.dtype)


def _swa(p, sinks, seq):
    t = p.shape[0]
    w = SWA_WINDOW
    blocks_per_seq = seq // w
    q_blk = P_QA // SWA_Q_W
    k_blk = P_KA // SWA_KV_W
    v_blk = P_VA // SWA_KV_W

    def prev(i):
        return jnp.maximum(i - 1, 0)

    return pl.pallas_call(
        functools.partial(_swa_kernel, blocks_per_seq=blocks_per_seq),
        out_shape=jax.ShapeDtypeStruct((t, SWA_Q_W), BF16),
        grid=(t // w,),
        in_specs=[
            pl.BlockSpec(memory_space=pltpu.SMEM),
            pl.BlockSpec((w, SWA_Q_W), lambda i: (i, q_blk)),
            pl.BlockSpec((w, SWA_KV_W), lambda i: (i, k_blk)),
            pl.BlockSpec((w, SWA_KV_W), lambda i: (i, v_blk)),
            pl.BlockSpec((w, SWA_KV_W), lambda i: (prev(i), k_blk)),
            pl.BlockSpec((w, SWA_KV_W), lambda i: (prev(i), v_blk)),
        ],
        out_specs=pl.BlockSpec((w, SWA_Q_W), lambda i: (i, 0)),
        compiler_params=pltpu.CompilerParams(dimension_semantics=("arbitrary",)),
        name="swa",
    )(sinks, p, p, p, p, p)


def _xattn_kernel(q_ref, mk_ref, mv_ref, o_ref):
    hd = XA_HEAD_DIM
    ones = jnp.ones((N_MEM, hd), BF16)
    sub = 128
    units = [(slice(r, r + sub), slice(h * hd, (h + 1) * hd))
             for r in range(0, q_ref.shape[0], sub) for h in range(XA_HEADS)]
    mv = [jnp.concatenate([mv_ref[:, h * hd:(h + 1) * hd], ones], axis=1) for h in range(XA_HEADS)]
    s = [_dot_nt(q_ref[rs, sl], mk_ref[:, sl]) * (hd ** -0.5) for rs, sl in units]
    e = [jnp.exp(si - jnp.max(si, axis=-1, keepdims=True)).astype(BF16) for si in s]
    oe = [_dot(ei, mv[i % XA_HEADS]) for i, ei in enumerate(e)]
    for (rs, sl), oi in zip(units, oe):
        o_ref[rs, sl] = (oi[:, :hd] / oi[:, hd:hd + 1]).astype(o_ref.dtype)


def _xattn(p, mkv, seq, *, tq=512):
    t = p.shape[0]
    tiles_per_seq = seq // tq
    q_blk = P_QC // XA_W
    return pl.pallas_call(
        _xattn_kernel,
        out_shape=jax.ShapeDtypeStruct((t, XA_W), BF16),
        grid=(t // tq,),
        in_specs=[
            pl.BlockSpec((tq, XA_W), lambda i: (i, q_blk)),
            pl.BlockSpec((N_MEM, XA_W), lambda i: (i // tiles_per_seq, 0)),
            pl.BlockSpec((N_MEM, XA_W), lambda i: (i // tiles_per_seq, 1)),
        ],
        out_specs=pl.BlockSpec((tq, XA_W), lambda i: (i, 0)),
        compiler_params=pltpu.CompilerParams(dimension_semantics=("arbitrary",)),
        name="xattn",
    )(p, mkv, mkv)


GDN_TS = 256
GDN_QKV_BLOCK = 256
GDN_QKV_BLOCKS = 3 * GDN_W // GDN_QKV_BLOCK
assert P_QB % GDN_QKV_BLOCK == 0
CONV_PAD = 8


def _gdn_kernel(*refs):
    qkv_refs = refs[:GDN_QKV_BLOCKS]
    (z_ref, ab_ref, cw_ref, alog_ref, dtb_ref, nw_ref, y_ref,
     state_ref, carry_ref, xpad_ref, cv_ref) = refs[GDN_QKV_BLOCKS:]
    ts = GDN_TS
    c = GDN_CHUNK
    dh = GDN_HEAD_DIM
    gw = GDN_W
    heads = range(GDN_HEADS)
    chunks = [slice(i * c, (i + 1) * c) for i in range(ts // c)]

    @pl.when(pl.program_id(1) == 0)
    def _():
        state_ref[...] = jnp.zeros_like(state_ref)
        carry_ref[...] = jnp.zeros_like(carry_ref)

    xpad_ref[0:CONV_PAD, :] = carry_ref[...]
    for i, r in enumerate(qkv_refs):
        xpad_ref[CONV_PAD:, i * GDN_QKV_BLOCK:(i + 1) * GDN_QKV_BLOCK] = r[...].astype(F32)
    carry_ref[...] = xpad_ref[ts:ts + CONV_PAD, :]
    for cb in range(3 * gw // LANES):
        cs = slice(cb * LANES, (cb + 1) * LANES)
        acc = None
        for i in range(GDN_CONV):
            off = CONV_PAD - (GDN_CONV - 1) + i
            term = cw_ref[i:i + 1, cs] * xpad_ref[off:off + ts, cs]
            acc = term if acc is None else acc + term
        cv_ref[:, cs] = acc * jax.nn.sigmoid(acc)

    ab = ab_ref[...]
    g = -jnp.exp(alog_ref[...]) * jax.nn.softplus(ab + dtb_ref[...])
    beta_all = jax.nn.sigmoid(ab)
    ri = lax.broadcasted_iota(jnp.int32, (ts, ts), 0)
    ci = lax.broadcasted_iota(jnp.int32, (ts, ts), 1)
    same_chunk = (ri // c) == (ci // c)
    causal = same_chunk & (ri >= ci)
    strict = same_chunk & (ri > ci)
    gcum = _dot_f32(causal.astype(F32), g)
    gcum_t = gcum.T
    g_last = jnp.concatenate(
        [jnp.broadcast_to(gcum[rc.stop - 1:rc.stop], (c, LANES)) for rc in chunks], axis=0)
    k_scale = jnp.exp(g_last - gcum)
    eg_all = jnp.exp(gcum)

    q, k, v, gc, beta = [], [], [], [], []
    for h in heads:
        qh = cv_ref[:, h * dh:(h + 1) * dh]
        kh = cv_ref[:, gw + h * dh:gw + (h + 1) * dh]
        q.append(qh * lax.rsqrt(jnp.sum(qh * qh, axis=-1, keepdims=True) + L2_EPS) * (dh ** -0.5))
        k.append(kh * lax.rsqrt(jnp.sum(kh * kh, axis=-1, keepdims=True) + L2_EPS))
        v.append(cv_ref[:, 2 * gw + h * dh:2 * gw + (h + 1) * dh])
        gc.append(gcum[:, h:h + 1])
        beta.append(beta_all[:, GDN_HEADS + h:GDN_HEADS + h + 1])
    qkk = [_dot_nt(jnp.concatenate([q[h], k[h]], axis=0).astype(BF16), k[h].astype(BF16)) for h in heads]
    decay = [jnp.exp(jnp.where(causal, gc[h] - gcum_t[h:h + 1, :], -jnp.inf)) for h in heads]
    qk = [qkk[h][:ts] * decay[h] for h in heads]
    m = [-jnp.where(strict, beta[h] * qkk[h][ts:] * decay[h], 0.0) for h in heads]
    sol = [jnp.concatenate([v[h] * beta[h], k[h] * (beta[h] * eg_all[:, h:h + 1])], axis=-1) for h in heads]
    n_rounds = c.bit_length() - 1
    for r in range(n_rounds):
        last = r + 1 == n_rounds
        mb = [m[h].astype(BF16) for h in heads]
        rhs = [sol[h].astype(BF16) if last else jnp.concatenate([sol[h], m[h]], axis=-1).astype(BF16)
               for h in heads]
        prod = [_dot(mb[h], rhs[h]) for h in heads]
        sol = [sol[h] + prod[h][:, :2 * dh] for h in heads]
        if not last:
            m = [prod[h][:, 2 * dh:] for h in heads]
    k_dec = [k[h] * k_scale[:, h:h + 1] for h in heads]
    q_dec = [q[h] * eg_all[:, h:h + 1] for h in heads]

    x = {}
    for ic, rc in enumerate(chunks):
        for h in heads:
            lhs = jnp.concatenate([k_dec[h][rc].T, qk[h][rc, rc]], axis=0).astype(BF16)
            x[ic, h] = _dot(lhs, sol[h][rc].astype(BF16))

    for ic, rc in enumerate(chunks):
        for h in heads:
            xs = x[ic, h]
            lhs = jnp.concatenate([-xs[:dh, dh:], q_dec[h][rc] - xs[dh:, dh:]], axis=0).astype(BF16)
            s = state_ref[h]
            y = _dot(lhs, s.astype(BF16))
            state_ref[h] = s * eg_all[rc.stop - 1:rc.stop, h:h + 1] + y[:dh] + xs[:dh, :dh]
            o = _rms(y[dh:] + xs[dh:, :dh], nw_ref[...])
            hs = slice(h * dh, (h + 1) * dh)
            z = z_ref[rc, hs].astype(F32)
            y_ref[rc, hs] = (o * (z * jax.nn.sigmoid(z))).astype(y_ref.dtype)


def _gdn(p, ab, conv_w, alog_row, dtb_row, norm_w, batch, seq):
    t = p.shape[0]
    ts = GDN_TS
    nj = seq // ts
    gw = GDN_W

    def rows(col_blk):
        return lambda b, j: (b * nj + j, col_blk)

    return pl.pallas_call(
        _gdn_kernel,
        out_shape=jax.ShapeDtypeStruct((t, gw), BF16),
        grid=(batch, nj),
        in_specs=[
            *[pl.BlockSpec((ts, GDN_QKV_BLOCK), rows(P_QB // GDN_QKV_BLOCK + i)) for i in range(GDN_QKV_BLOCKS)],
            pl.BlockSpec((ts, gw), rows(P_Z // gw)),
            pl.BlockSpec((ts, LANES), rows(0)),
            pl.BlockSpec((GDN_CONV, 3 * gw), lambda b, j: (0, 0)),
            pl.BlockSpec((1, LANES), lambda b, j: (0, 0)),
            pl.BlockSpec((1, LANES), lambda b, j: (0, 0)),
            pl.BlockSpec((1, GDN_HEAD_DIM), lambda b, j: (0, 0)),
        ],
        out_specs=pl.BlockSpec((ts, gw), rows(0)),
        scratch_shapes=[
            pltpu.VMEM((GDN_HEADS, GDN_HEAD_DIM, GDN_HEAD_DIM), F32),
            pltpu.VMEM((CONV_PAD, 3 * gw), F32),
            pltpu.VMEM((ts + CONV_PAD, 3 * gw), F32),
            pltpu.VMEM((ts, 3 * gw), F32),
        ],
        compiler_params=pltpu.CompilerParams(dimension_semantics=("arbitrary", "arbitrary")),
        name="gdn",
    )(*([p] * GDN_QKV_BLOCKS), p, ab, conv_w, alog_row, dtb_row, norm_w)


def _merge_kernel(x_ref, ga_ref, gb_ref, gc_ref, ya_ref, yb_ref, yc_ref, wa_ref, wb_ref, wc_ref, wo_ref,
                  h_ref):
    merged = jax.nn.sigmoid(ga_ref[...].astype(F32)) * _dot(ya_ref[...], wa_ref[...])
    merged += jax.nn.sigmoid(gb_ref[...].astype(F32)) * _dot(yb_ref[...], wb_ref[...])
    merged += jax.nn.sigmoid(gc_ref[...].astype(F32)) * _dot(yc_ref[...], wc_ref[...])
    h_ref[...] = x_ref[...] + _dot(merged.astype(BF16), wo_ref[...])


def _merge_out(x, p, ya, yb, yc, wa, wb, wc, wo, *, tm=256):
    t, d = x.shape

    def resident(shape):
        return pl.BlockSpec(shape, lambda i: (0, 0), pipeline_mode=pl.Buffered(1))

    return pl.pallas_call(
        _merge_kernel,
        out_shape=jax.ShapeDtypeStruct((t, d), F32),
        grid=(t // tm,),
        in_specs=[
            pl.BlockSpec((tm, d), lambda i: (i, 0)),
            pl.BlockSpec((tm, d), lambda i: (i, P_GATE // d)),
            pl.BlockSpec((tm, d), lambda i: (i, P_GATE // d + 1)),
            pl.BlockSpec((tm, d), lambda i: (i, P_GATE // d + 2)),
            pl.BlockSpec((tm, SWA_Q_W), lambda i: (i, 0)),
            pl.BlockSpec((tm, GDN_W), lambda i: (i, 0)),
            pl.BlockSpec((tm, XA_W), lambda i: (i, 0)),
            resident((SWA_Q_W, d)),
            resident((GDN_W, d)),
            resident((XA_W, d)),
            resident((d, d)),
        ],
        out_specs=pl.BlockSpec((tm, d), lambda i: (i, 0)),
        compiler_params=pltpu.CompilerParams(
            dimension_semantics=("arbitrary",), vmem_limit_bytes=VMEM_LIMIT),
        name="merge_out",
    )(x, p, p, p, ya, yb, yc, wa, wb, wc, wo)


def _mlp_kernel(h_ref, g_ref, w1_ref, w2_ref, gf_ref, o_ref, n_ref, acc_ref, *, final_norm):
    j = pl.program_id(1)

    @pl.when(j == 0)
    def _():
        h = h_ref[...]
        n_ref[...] = _rms(h, g_ref[...]).astype(BF16)
        acc_ref[...] = h

    u = _dot(n_ref[...], w1_ref[...])
    acc_ref[...] += _dot(jnp.square(jnp.maximum(u, 0.0)).astype(BF16), w2_ref[...])

    @pl.when(j == pl.num_programs(1) - 1)
    def _():
        h = acc_ref[...]
        o_ref[...] = _rms(h, gf_ref[...]) if final_norm else h


def _mlp(h, g, w1, w2, g_final, *, final_norm, tm=512, tf=1024):
    t, d = h.shape
    f = w1.shape[1]
    return pl.pallas_call(
        functools.partial(_mlp_kernel, final_norm=final_norm),
        out_shape=jax.ShapeDtypeStruct((t, d), F32),
        grid=(t // tm, f // tf),
        in_specs=[
            pl.BlockSpec((tm, d), lambda i, j: (i, 0)),
            pl.BlockSpec((1, d), lambda i, j: (0, 0)),
            pl.BlockSpec((d, tf), lambda i, j: (0, j)),
            pl.BlockSpec((tf, d), lambda i, j: (j, 0)),
            pl.BlockSpec((1, d), lambda i, j: (0, 0)),
        ],
        out_specs=pl.BlockSpec((tm, d), lambda i, j: (i, 0)),
        scratch_shapes=[pltpu.VMEM((tm, d), BF16), pltpu.VMEM((tm, d), F32)],
        compiler_params=pltpu.CompilerParams(
            dimension_semantics=("arbitrary", "arbitrary"), vmem_limit_bytes=VMEM_LIMIT),
        name="mlp",
    )(h, g, w1, w2, g_final)


def _lane_row(v):
    return jnp.zeros((1, LANES), F32).at[0, :v.shape[0]].set(v.astype(F32))


def kernel(x, mem, g_mix, w_in, sinks, conv_w, a_log, dt_bias, gdn_norm_w, g_mem, w_mem_kv, w_swa_up,
           w_gdn_up, w_xa_up, w_out, g_mlp, w_mlp_in, w_mlp_out, g_final):
    batch, seq, d = x.shape
    depth = w_in.shape[0]
    h = x.reshape(batch * seq, d)
    mem2 = mem.reshape(batch * N_MEM, d)
    for l in range(depth):
        p, ab = _in_proj(h, g_mix[l][None], jnp.swapaxes(w_in[l], 0, 1).astype(BF16))
        mkv = _mem_kv(mem2, g_mem[l][None], w_mem_kv[l].astype(BF16))
        ya = _swa(p, sinks[l], seq)
        yb = _gdn(p, ab, conv_w[l], _lane_row(a_log[l]), _lane_row(dt_bias[l]), gdn_norm_w[l][None],
                  batch, seq)
        yc = _xattn(p, mkv, seq)
        h = _merge_out(h, p, ya, yb, yc, w_swa_up[l].astype(BF16), w_gdn_up[l].astype(BF16),
                       w_xa_up[l].astype(BF16), w_out[l].astype(BF16))
        h = _mlp(h, g_mlp[l][None], w_mlp_in[l].astype(BF16), w_mlp_out[l].astype(BF16), g_final[None],
                 final_norm=(l == depth - 1))
    return h.reshape(batch, seq, d)
```

```python
import functools

import jax
import jax.numpy as jnp
from jax import lax
from jax.experimental import pallas as pl
from jax.experimental.pallas import tpu as pltpu

F32 = jnp.float32
BF16 = jnp.bfloat16

D_MODEL = 2048
SWA_Q_HEADS = 16
SWA_KV_HEADS = 2
SWA_HEAD_DIM = 64
SWA_WINDOW = 128
GDN_HEADS = 4
GDN_HEAD_DIM = 128
GDN_CONV = 4
GDN_CHUNK = 64
N_MEM = 256
XA_HEADS = 4
XA_HEAD_DIM = 128
D_FF = 4 * D_MODEL
RMS_EPS = 1e-6
L2_EPS = 1e-6

SWA_Q_W = SWA_Q_HEADS * SWA_HEAD_DIM
SWA_KV_W = SWA_KV_HEADS * SWA_HEAD_DIM
GDN_W = GDN_HEADS * GDN_HEAD_DIM
XA_W = XA_HEADS * XA_HEAD_DIM

LANES = 128

AB_W = 2 * GDN_HEADS

R_QA = 0
R_KA = R_QA + SWA_Q_W
R_VA = R_KA + SWA_KV_W
R_QB = R_VA + SWA_KV_W
R_AB = R_QB + 3 * GDN_W
R_Z = R_AB + AB_W
R_QC = R_Z + GDN_W
R_GATE = R_QC + XA_W
R_END = R_GATE + 3 * D_MODEL

IN_PROJ_TN = 1024
IN_PROJ_TILE_STARTS = (
    tuple(R_GATE + k * IN_PROJ_TN for k in range(3 * D_MODEL // IN_PROJ_TN))
    + tuple(range(0, R_Z, IN_PROJ_TN))
    + tuple(R_Z + k * IN_PROJ_TN for k in range((R_GATE - R_Z) // IN_PROJ_TN)))
P_WIDTH = len(IN_PROJ_TILE_STARTS) * IN_PROJ_TN
P_GATE = 0
P_FRONT = 3 * D_MODEL
P_QA = P_FRONT + R_QA
P_KA = P_FRONT + R_KA
P_VA = P_FRONT + R_VA
P_QB = P_FRONT + R_QB
P_AB = P_FRONT + R_AB
P_Z = P_FRONT + -(-R_Z // IN_PROJ_TN) * IN_PROJ_TN
P_QC = P_Z + GDN_W
assert (R_GATE - R_Z) % IN_PROJ_TN == 0 and (3 * D_MODEL) % IN_PROJ_TN == 0
assert all(s % 8 == 0 for s in IN_PROJ_TILE_STARTS) and P_AB % LANES == 0 and P_QC + XA_W == P_WIDTH

VMEM_LIMIT = 56 * 1024 * 1024


def _rms(x, g):
    return x * lax.rsqrt(jnp.mean(x * x, axis=-1, keepdims=True) + RMS_EPS) * g


def _dot(a, b):
    return jnp.dot(a, b, preferred_element_type=F32)


def _dot_nt(a, b):
    return lax.dot_general(a, b, (((1,), (1,)), ((), ())), preferred_element_type=F32)


def _dot_f32(a, b):
    return jnp.dot(a, b, preferred_element_type=F32, precision=lax.Precision.HIGHEST)


def _in_proj_kernel(starts_ref, x_ref, g_ref, wt_ref, o_ref, ab_ref, n_ref, *, ab_tile, ab_off):
    del starts_ref
    j = pl.program_id(1)

    @pl.when(j == 0)
    def _():
        n_ref[...] = _rms(x_ref[...], g_ref[...]).astype(BF16)

    acc = _dot_nt(n_ref[...], wt_ref[...])
    o_ref[...] = acc.astype(o_ref.dtype)

    @pl.when(j == ab_tile)
    def _():
        ab_ref[...] = acc[:, ab_off:ab_off + LANES]


def _in_proj(x, g, wt, *, tm=1024, tn=IN_PROJ_TN):
    t, d = x.shape
    return pl.pallas_call(
        functools.partial(_in_proj_kernel, ab_tile=P_AB // tn, ab_off=P_AB % tn),
        out_shape=(jax.ShapeDtypeStruct((t, P_WIDTH), BF16), jax.ShapeDtypeStruct((t, LANES), F32)),
        grid_spec=pltpu.PrefetchScalarGridSpec(
            num_scalar_prefetch=1,
            grid=(t // tm, P_WIDTH // tn),
            in_specs=[
                pl.BlockSpec((tm, d), lambda i, j, starts: (i, 0)),
                pl.BlockSpec((1, d), lambda i, j, starts: (0, 0)),
                pl.BlockSpec((pl.Element(tn), pl.Element(d)),
                             lambda i, j, starts: (pl.multiple_of(starts[j], 8), 0)),
            ],
            out_specs=(pl.BlockSpec((tm, tn), lambda i, j, starts: (i, j)),
                       pl.BlockSpec((tm, LANES), lambda i, j, starts: (i, 0))),
            scratch_shapes=[pltpu.VMEM((tm, d), BF16)],
        ),
        compiler_params=pltpu.CompilerParams(
            dimension_semantics=("arbitrary", "arbitrary"), vmem_limit_bytes=VMEM_LIMIT),
        name="in_proj",
    )(jnp.asarray(IN_PROJ_TILE_STARTS, jnp.int32), x, g, wt)


def _mem_kv_kernel(m_ref, g_ref, w_ref, o_ref):
    o_ref[...] = _dot(_rms(m_ref[...], g_ref[...]).astype(BF16), w_ref[...]).astype(o_ref.dtype)


def _mem_kv(mem, g, w):
    t, d = mem.shape
    n = w.shape[1]
    return pl.pallas_call(
        _mem_kv_kernel,
        out_shape=jax.ShapeDtypeStruct((t, n), BF16),
        grid=(t // N_MEM,),
        in_specs=[
            pl.BlockSpec((N_MEM, d), lambda i: (i, 0)),
            pl.BlockSpec((1, d), lambda i: (0, 0)),
            pl.BlockSpec((d, n), lambda i: (0, 0)),
        ],
        out_specs=pl.BlockSpec((N_MEM, n), lambda i: (i, 0)),
        compiler_params=pltpu.CompilerParams(
            dimension_semantics=("arbitrary",), vmem_limit_bytes=VMEM_LIMIT),
        name="mem_kv",
    )(mem, g, w)


def _swa_block(sinks_ref, q_ref, rows, k, v, first, o_ref):
    w = SWA_WINDOW
    hd = SWA_HEAD_DIM
    group = SWA_Q_HEADS // SWA_KV_HEADS
    ones = jnp.ones((2 * w, hd), BF16)
    qi = lax.broadcasted_iota(jnp.int32, (w, 2 * w), 0)
    kj = lax.broadcasted_iota(jnp.int32, (w, 2 * w), 1)
    valid = (kj > qi) & (kj <= qi + w)
    if first is not False:
        valid = valid & (kj >= jnp.where(first, w, 0))
    heads = range(SWA_Q_HEADS)
    kh = [k[:, hk * hd:(hk + 1) * hd] for hk in range(SWA_KV_HEADS)]
    vh = [jnp.concatenate([v[:, hk * hd:(hk + 1) * hd], ones], axis=1) for hk in range(SWA_KV_HEADS)]
    s = [_dot_nt(q_ref[rows, h * hd:(h + 1) * hd], kh[h // group]) for h in heads]
    s = [jnp.where(valid, s[h], -jnp.inf) for h in heads]
    m = [jnp.maximum(jnp.max(s[h], axis=-1, keepdims=True), sinks_ref[h]) for h in heads]
    p = [jnp.exp(s[h] - m[h]).astype(BF16) for h in heads]
    ov = [_dot(p[h], vh[h // group]) for h in heads]
    for h in heads:
        o = ov[h][:, :hd] / (ov[h][:, hd:hd + 1] + jnp.exp(sinks_ref[h] - m[h]))
        o_ref[rows, h * hd:(h + 1) * hd] = o.astype(o_ref.dtype)


XA_SUB = 128


def _xattn_rows(q_ref, mk_ref, mv_ref, o_ref, col0):
    hd = XA_HEAD_DIM
    ones = jnp.ones((N_MEM, hd), BF16)
    units = [(slice(r, r + XA_SUB), h) for r in range(0, q_ref.shape[0], XA_SUB) for h in range(XA_HEADS)]
    mv = [jnp.concatenate([mv_ref[:, h * hd:(h + 1) * hd], ones], axis=1) for h in range(XA_HEADS)]
    s = [_dot_nt(q_ref[rs, h * hd:(h + 1) * hd], mk_ref[:, h * hd:(h + 1) * hd]) * (hd ** -0.5)
         for rs, h in units]
    e = [jnp.exp(si - jnp.max(si, axis=-1, keepdims=True)).astype(BF16) for si in s]
    oe = [_dot(ei, mv[h]) for ei, (_, h) in zip(e, units)]
    for (rs, h), oi in zip(units, oe):
        o_ref[rs, col0 + h * hd:col0 + (h + 1) * hd] = (oi[:, :hd] / oi[:, hd:hd + 1]).astype(o_ref.dtype)


GDN_TS = 256
GDN_QKV_BLOCK = 256
GDN_QKV_BLOCKS = 3 * GDN_W // GDN_QKV_BLOCK
assert P_QB % GDN_QKV_BLOCK == 0
CONV_PAD = 8


def _gdn_kernel(*refs):
    qkv_refs = refs[:GDN_QKV_BLOCKS]
    (z_ref, ab_ref, cw_ref, alog_ref, dtb_ref, nw_ref, y_ref,
     state_ref, carry_ref, xpad_ref, cv_ref) = refs[GDN_QKV_BLOCKS:]
    ts = GDN_TS
    c = GDN_CHUNK
    dh = GDN_HEAD_DIM
    gw = GDN_W
    heads = range(GDN_HEADS)
    chunks = [slice(i * c, (i + 1) * c) for i in range(ts // c)]

    @pl.when(pl.program_id(1) == 0)
    def _():
        state_ref[...] = jnp.zeros_like(state_ref)
        carry_ref[...] = jnp.zeros_like(carry_ref)

    xpad_ref[0:CONV_PAD, :] = carry_ref[...]
    for i, r in enumerate(qkv_refs):
        xpad_ref[CONV_PAD:, i * GDN_QKV_BLOCK:(i + 1) * GDN_QKV_BLOCK] = r[...].astype(F32)
    carry_ref[...] = xpad_ref[ts:ts + CONV_PAD, :]
    for cb in range(3 * gw // LANES):
        cs = slice(cb * LANES, (cb + 1) * LANES)
        acc = None
        for i in range(GDN_CONV):
            off = CONV_PAD - (GDN_CONV - 1) + i
            term = cw_ref[i:i + 1, cs] * xpad_ref[off:off + ts, cs]
            acc = term if acc is None else acc + term
        cv_ref[:, cs] = acc * jax.nn.sigmoid(acc)

    ab = ab_ref[...]
    g = -jnp.exp(alog_ref[...]) * jax.nn.softplus(ab + dtb_ref[...])
    beta_all = jax.nn.sigmoid(ab)
    ri = lax.broadcasted_iota(jnp.int32, (ts, ts), 0)
    ci = lax.broadcasted_iota(jnp.int32, (ts, ts), 1)
    same_chunk = (ri // c) == (ci // c)
    causal = same_chunk & (ri >= ci)
    strict = same_chunk & (ri > ci)
    gcum = _dot_f32(causal.astype(F32), g)
    gcum_t = gcum.T
    g_last = jnp.concatenate(
        [jnp.broadcast_to(gcum[rc.stop - 1:rc.stop], (c, LANES)) for rc in chunks], axis=0)
    k_scale = jnp.exp(g_last - gcum)
    eg_all = jnp.exp(gcum)

    q, k, v, gc, beta = [], [], [], [], []
    for h in heads:
        qh = cv_ref[:, h * dh:(h + 1) * dh]
        kh = cv_ref[:, gw + h * dh:gw + (h + 1) * dh]
        q.append(qh * lax.rsqrt(jnp.sum(qh * qh, axis=-1, keepdims=True) + L2_EPS) * (dh ** -0.5))
        k.append(kh * lax.rsqrt(jnp.sum(kh * kh, axis=-1, keepdims=True) + L2_EPS))
        v.append(cv_ref[:, 2 * gw + h * dh:2 * gw + (h + 1) * dh])
        gc.append(gcum[:, h:h + 1])
        beta.append(beta_all[:, GDN_HEADS + h:GDN_HEADS + h + 1])
    qkk = [_dot_nt(jnp.concatenate([q[h], k[h]], axis=0).astype(BF16), k[h].astype(BF16)) for h in heads]
    decay = [jnp.exp(jnp.where(causal, gc[h] - gcum_t[h:h + 1, :], -jnp.inf)) for h in heads]
    qk = [qkk[h][:ts] * decay[h] for h in heads]
    m = [-jnp.where(strict, beta[h] * qkk[h][ts:] * decay[h], 0.0) for h in heads]
    sol = [jnp.concatenate([v[h] * beta[h], k[h] * (beta[h] * eg_all[:, h:h + 1])], axis=-1) for h in heads]
    n_rounds = c.bit_length() - 1
    for r in range(n_rounds):
        last = r + 1 == n_rounds
        mb = [m[h].astype(BF16) for h in heads]
        rhs = [sol[h].astype(BF16) if last else jnp.concatenate([sol[h], m[h]], axis=-1).astype(BF16)
               for h in heads]
        prod = [_dot(mb[h], rhs[h]) for h in heads]
        sol = [sol[h] + prod[h][:, :2 * dh] for h in heads]
        if not last:
            m = [prod[h][:, 2 * dh:] for h in heads]
    k_dec = [k[h] * k_scale[:, h:h + 1] for h in heads]
    q_dec = [q[h] * eg_all[:, h:h + 1] for h in heads]

    x = {}
    for ic, rc in enumerate(chunks):
        for h in heads:
            lhs = jnp.concatenate([k_dec[h][rc].T, qk[h][rc, rc]], axis=0).astype(BF16)
            x[ic, h] = _dot(lhs, sol[h][rc].astype(BF16))

    for ic, rc in enumerate(chunks):
        for h in heads:
            xs = x[ic, h]
            lhs = jnp.concatenate([-xs[:dh, dh:], q_dec[h][rc] - xs[dh:, dh:]], axis=0).astype(BF16)
            s = state_ref[h]
            y = _dot(lhs, s.astype(BF16))
            state_ref[h] = s * eg_all[rc.stop - 1:rc.stop, h:h + 1] + y[:dh] + xs[:dh, :dh]
            o = _rms(y[dh:] + xs[dh:, :dh], nw_ref[...])
            hs = slice(h * dh, (h + 1) * dh)
            z = z_ref[rc, hs].astype(F32)
            y_ref[rc, hs] = (o * (z * jax.nn.sigmoid(z))).astype(y_ref.dtype)


def _gdn(p, ab, conv_w, alog_row, dtb_row, norm_w, batch, seq):
    t = p.shape[0]
    ts = GDN_TS
    nj = seq // ts
    gw = GDN_W

    def rows(col_blk):
        return lambda b, j: (b * nj + j, col_blk)

    return pl.pallas_call(
        _gdn_kernel,
        out_shape=jax.ShapeDtypeStruct((t, gw), BF16),
        grid=(batch, nj),
        in_specs=[
            *[pl.BlockSpec((ts, GDN_QKV_BLOCK), rows(P_QB // GDN_QKV_BLOCK + i)) for i in range(GDN_QKV_BLOCKS)],
            pl.BlockSpec((ts, gw), rows(P_Z // gw)),
            pl.BlockSpec((ts, LANES), rows(0)),
            pl.BlockSpec((GDN_CONV, 3 * gw), lambda b, j: (0, 0)),
            pl.BlockSpec((1, LANES), lambda b, j: (0, 0)),
            pl.BlockSpec((1, LANES), lambda b, j: (0, 0)),
            pl.BlockSpec((1, GDN_HEAD_DIM), lambda b, j: (0, 0)),
        ],
        out_specs=pl.BlockSpec((ts, gw), rows(0)),
        scratch_shapes=[
            pltpu.VMEM((GDN_HEADS, GDN_HEAD_DIM, GDN_HEAD_DIM), F32),
            pltpu.VMEM((CONV_PAD, 3 * gw), F32),
            pltpu.VMEM((ts + CONV_PAD, 3 * gw), F32),
            pltpu.VMEM((ts, 3 * gw), F32),
        ],
        compiler_params=pltpu.CompilerParams(dimension_semantics=("arbitrary", "arbitrary")),
        name="gdn",
    )(*([p] * GDN_QKV_BLOCKS), p, ab, conv_w, alog_row, dtb_row, norm_w)


def _mix_kernel(sinks_ref, q_ref, kv_ref, kvp_ref, qc_ref, mk_ref, mv_ref, x_ref, ga_ref, gb_ref, gc_ref, yb_ref,
                wa_ref, wb_ref, wc_ref, wo_ref, h_ref, y0_ref, y1_ref, *, tiles_per_seq, n_tiles):
    i = pl.program_id(0)
    w = SWA_WINDOW
    hd = SWA_HEAD_DIM

    @pl.when(i == 0)
    def _():
        y0_ref[...] = jnp.zeros_like(y0_ref)

    def step(yprev_ref, ynext_ref):
        first = (jnp.minimum(i, n_tiles - 1) % tiles_per_seq) == 0
        kvw = SWA_KV_W
        k_all = (jnp.concatenate([kvp_ref[:, :kvw], kv_ref[:, :kvw]], axis=0).astype(F32)
                 * (hd ** -0.5)).astype(BF16)
        v_all = jnp.concatenate([kvp_ref[:, kvw:], kv_ref[:, kvw:]], axis=0)
        for b in range(q_ref.shape[0] // w):
            _swa_block(sinks_ref, q_ref, slice(b * w, (b + 1) * w), k_all[b * w:(b + 2) * w],
                       v_all[b * w:(b + 2) * w], first if b == 0 else False, ynext_ref)
        _xattn_rows(qc_ref, mk_ref, mv_ref, ynext_ref, SWA_Q_W)

        merged = jax.nn.sigmoid(ga_ref[...].astype(F32)) * _dot(yprev_ref[:, :SWA_Q_W], wa_ref[...])
        merged += jax.nn.sigmoid(gb_ref[...].astype(F32)) * _dot(yb_ref[...], wb_ref[...])
        merged += jax.nn.sigmoid(gc_ref[...].astype(F32)) * _dot(yprev_ref[:, SWA_Q_W:], wc_ref[...])
        h_ref[...] = x_ref[...] + _dot(merged.astype(BF16), wo_ref[...])

    @pl.when(i % 2 == 0)
    def _():
        step(y0_ref, y1_ref)

    @pl.when(i % 2 == 1)
    def _():
        step(y1_ref, y0_ref)


def _mix(x, p, mkv, yb, sinks, wa, wb, wc, wo, seq, *, tm=256):
    t, d = x.shape
    w = SWA_WINDOW
    n_tiles = t // tm
    tiles_per_seq = seq // tm
    kv_w = 2 * SWA_KV_W
    assert P_VA == P_KA + SWA_KV_W and P_KA % kv_w == 0

    def att(i):
        return jnp.minimum(i, n_tiles - 1)

    def mm(i):
        return jnp.maximum(i - 1, 0)

    def resident(shape):
        return pl.BlockSpec(shape, lambda i: (0, 0), pipeline_mode=pl.Buffered(1))

    return pl.pallas_call(
        functools.partial(_mix_kernel, tiles_per_seq=tiles_per_seq, n_tiles=n_tiles),
        out_shape=jax.ShapeDtypeStruct((t, d), F32),
        grid=(n_tiles + 1,),
        in_specs=[
            pl.BlockSpec(memory_space=pltpu.SMEM),
            pl.BlockSpec((tm, SWA_Q_W), lambda i: (att(i), P_QA // SWA_Q_W)),
            pl.BlockSpec((tm, kv_w), lambda i: (att(i), P_KA // kv_w)),
            pl.BlockSpec((w, kv_w), lambda i: (jnp.maximum(att(i) * (tm // w) - 1, 0), P_KA // kv_w)),
            pl.BlockSpec((tm, XA_W), lambda i: (att(i), P_QC // XA_W)),
            pl.BlockSpec((N_MEM, XA_W), lambda i: (att(i) // tiles_per_seq, 0)),
            pl.BlockSpec((N_MEM, XA_W), lambda i: (att(i) // tiles_per_seq, 1)),
            pl.BlockSpec((tm, d), lambda i: (mm(i), 0)),
            pl.BlockSpec((tm, d), lambda i: (mm(i), P_GATE // d)),
            pl.BlockSpec((tm, d), lambda i: (mm(i), P_GATE // d + 1)),
            pl.BlockSpec((tm, d), lambda i: (mm(i), P_GATE // d + 2)),
            pl.BlockSpec((tm, GDN_W), lambda i: (mm(i), 0)),
            resident((SWA_Q_W, d)),
            resident((GDN_W, d)),
            resident((XA_W, d)),
            resident((d, d)),
        ],
        out_specs=pl.BlockSpec((tm, d), lambda i: (mm(i), 0)),
        scratch_shapes=[pltpu.VMEM((tm, SWA_Q_W + XA_W), BF16), pltpu.VMEM((tm, SWA_Q_W + XA_W), BF16)],
        compiler_params=pltpu.CompilerParams(
            dimension_semantics=("arbitrary",), vmem_limit_bytes=VMEM_LIMIT),
        name="mix",
    )(sinks, p, p, p, p, mkv, mkv, x, p, p, p, yb, wa, wb, wc, wo)


def _mlp_kernel(h_ref, g_ref, w1_ref, w2_ref, gf_ref, o_ref, n_ref, acc_ref, *, final_norm):
    j = pl.program_id(1)

    @pl.when(j == 0)
    def _():
        h = h_ref[...]
        n_ref[...] = _rms(h, g_ref[...]).astype(BF16)
        acc_ref[...] = h

    u = _dot(n_ref[...], w1_ref[...])
    acc_ref[...] += _dot(jnp.square(jnp.maximum(u, 0.0)).astype(BF16), w2_ref[...])

    @pl.when(j == pl.num_programs(1) - 1)
    def _():
        h = acc_ref[...]
        o_ref[...] = _rms(h, gf_ref[...]) if final_norm else h


def _mlp(h, g, w1, w2, g_final, *, final_norm, tm=512, tf=1024):
    t, d = h.shape
    f = w1.shape[1]
    return pl.pallas_call(
        functools.partial(_mlp_kernel, final_norm=final_norm),
        out_shape=jax.ShapeDtypeStruct((t, d), F32),
        grid=(t // tm, f // tf),
        in_specs=[
            pl.BlockSpec((tm, d), lambda i, j: (i, 0)),
            pl.BlockSpec((1, d), lambda i, j: (0, 0)),
            pl.BlockSpec((d, tf), lambda i, j: (0, j)),
            pl.BlockSpec((tf, d), lambda i, j: (j, 0)),
            pl.BlockSpec((1, d), lambda i, j: (0, 0)),
        ],
        out_specs=pl.BlockSpec((tm, d), lambda i, j: (i, 0)),
        scratch_shapes=[pltpu.VMEM((tm, d), BF16), pltpu.VMEM((tm, d), F32)],
        compiler_params=pltpu.CompilerParams(
            dimension_semantics=("arbitrary", "arbitrary"), vmem_limit_bytes=VMEM_LIMIT),
        name="mlp",
    )(h, g, w1, w2, g_final)


def _lane_row(v):
    return jnp.zeros((1, LANES), F32).at[0, :v.shape[0]].set(v.astype(F32))


def kernel(x, mem, g_mix, w_in, sinks, conv_w, a_log, dt_bias, gdn_norm_w, g_mem, w_mem_kv, w_swa_up,
           w_gdn_up, w_xa_up, w_out, g_mlp, w_mlp_in, w_mlp_out, g_final):
    batch, seq, d = x.shape
    depth = w_in.shape[0]
    h = x.reshape(batch * seq, d)
    mem2 = mem.reshape(batch * N_MEM, d)
    for l in range(depth):
        p, ab = _in_proj(h, g_mix[l][None], jnp.swapaxes(w_in[l], 0, 1).astype(BF16))
        mkv = _mem_kv(mem2, g_mem[l][None], w_mem_kv[l].astype(BF16))
        yb = _gdn(p, ab, conv_w[l], _lane_row(a_log[l]), _lane_row(dt_bias[l]), gdn_norm_w[l][None],
                  batch, seq)
        h = _mix(h, p, mkv, yb, sinks[l], w_swa_up[l].astype(BF16), w_gdn_up[l].astype(BF16),
                 w_xa_up[l].astype(BF16), w_out[l].astype(BF16), seq)
        h = _mlp(h, g_mlp[l][None], w_mlp_in[l].astype(BF16), w_mlp_out[l].astype(BF16), g_final[None],
                 final_norm=(l == depth - 1))
    return h.reshape(batch, seq, d)
```

```python
import functools

import jax
import jax.numpy as jnp
from jax import lax
from jax.experimental import pallas as pl
from jax.experimental.pallas import tpu as pltpu

F32 = jnp.float32
BF16 = jnp.bfloat16

D_MODEL = 2048
SWA_Q_HEADS = 16
SWA_KV_HEADS = 2
SWA_HEAD_DIM = 64
SWA_WINDOW = 128
GDN_HEADS = 4
GDN_HEAD_DIM = 128
GDN_CONV = 4
GDN_CHUNK = 64
N_MEM = 256
XA_HEADS = 4
XA_HEAD_DIM = 128
D_FF = 4 * D_MODEL
RMS_EPS = 1e-6
L2_EPS = 1e-6

SWA_Q_W = SWA_Q_HEADS * SWA_HEAD_DIM
SWA_KV_W = SWA_KV_HEADS * SWA_HEAD_DIM
GDN_W = GDN_HEADS * GDN_HEAD_DIM
XA_W = XA_HEADS * XA_HEAD_DIM

LANES = 128

AB_W = 2 * GDN_HEADS

R_QA = 0
R_KA = R_QA + SWA_Q_W
R_VA = R_KA + SWA_KV_W
R_QB = R_VA + SWA_KV_W
R_AB = R_QB + 3 * GDN_W
R_Z = R_AB + AB_W
R_QC = R_Z + GDN_W
R_GATE = R_QC + XA_W
R_END = R_GATE + 3 * D_MODEL

IN_PROJ_TN = 1024
IN_PROJ_TILE_STARTS = (
    tuple(R_GATE + k * IN_PROJ_TN for k in range(3 * D_MODEL // IN_PROJ_TN))
    + tuple(range(0, R_Z, IN_PROJ_TN))
    + tuple(R_Z + k * IN_PROJ_TN for k in range((R_GATE - R_Z) // IN_PROJ_TN)))
P_WIDTH = len(IN_PROJ_TILE_STARTS) * IN_PROJ_TN
P_GATE = 0
P_FRONT = 3 * D_MODEL
P_QA = P_FRONT + R_QA
P_KA = P_FRONT + R_KA
P_VA = P_FRONT + R_VA
P_QB = P_FRONT + R_QB
P_AB = P_FRONT + R_AB
P_Z = P_FRONT + -(-R_Z // IN_PROJ_TN) * IN_PROJ_TN
P_QC = P_Z + GDN_W
assert (R_GATE - R_Z) % IN_PROJ_TN == 0 and (3 * D_MODEL) % IN_PROJ_TN == 0
assert all(s % 8 == 0 for s in IN_PROJ_TILE_STARTS) and P_AB % LANES == 0 and P_QC + XA_W == P_WIDTH

VMEM_LIMIT = 56 * 1024 * 1024


def _rms(x, g):
    return x * lax.rsqrt(jnp.mean(x * x, axis=-1, keepdims=True) + RMS_EPS) * g


def _dot(a, b):
    return jnp.dot(a, b, preferred_element_type=F32)


def _dot_nt(a, b):
    return lax.dot_general(a, b, (((1,), (1,)), ((), ())), preferred_element_type=F32)


def _dot_f32(a, b):
    return jnp.dot(a, b, preferred_element_type=F32, precision=lax.Precision.HIGHEST)


CAST_J = 8


def _in_proj_kernel(starts_ref, x_ref, g_ref, wt_ref, w1_ref, w2_ref, wo_ref, wa_ref, wb_ref, wc_ref, wm_ref,
                    o_ref, ab_ref, w1b_ref, w2b_ref, wob_ref, wab_ref, wbb_ref, wcb_ref, wmb_ref, n_ref, *,
                    ab_tile, ab_off):
    del starts_ref
    j = pl.program_id(1)

    @pl.when(j == 0)
    def _():
        n_ref[...] = _rms(x_ref[...], g_ref[...]).astype(BF16)

    acc = _dot_nt(n_ref[...], wt_ref[...])
    o_ref[...] = acc.astype(o_ref.dtype)

    @pl.when(j < CAST_J)
    def _():
        w1b_ref[...] = w1_ref[...].astype(BF16)
        w2b_ref[...] = w2_ref[...].astype(BF16)

    @pl.when(j == CAST_J)
    def _():
        wob_ref[...] = wo_ref[...].astype(BF16)
        wab_ref[...] = wa_ref[...].astype(BF16)

    @pl.when(j == CAST_J + 1)
    def _():
        wbb_ref[...] = wb_ref[...].astype(BF16)
        wcb_ref[...] = wc_ref[...].astype(BF16)
        wmb_ref[...] = wm_ref[...].astype(BF16)

    @pl.when(j == ab_tile)
    def _():
        ab_ref[...] = acc[:, ab_off:ab_off + LANES]


def _in_proj(x, g, wt, w1, w2, wo, wa, wb, wc, wm, *, tm=1024, tn=IN_PROJ_TN):
    t, d = x.shape
    n_row, n_col = t // tm, P_WIDTH // tn
    assert n_col == CAST_J + 2

    def per_step(w):
        r, c = w.shape
        assert r % (n_row * 16) == 0 and c % (CAST_J * LANES) == 0
        return pl.BlockSpec((r // n_row, c // CAST_J), lambda i, j, starts: (i, jnp.minimum(j, CAST_J - 1)))

    def per_row_tile(w):
        r, c = w.shape
        assert r % (n_row * 16) == 0
        return pl.BlockSpec((r // n_row, c), lambda i, j, starts: (i, 0))

    weights = (w1, w2, wo, wa, wb, wc, wm)
    w_specs = [per_step(w1), per_step(w2)] + [per_row_tile(w) for w in weights[2:]]
    return pl.pallas_call(
        functools.partial(_in_proj_kernel, ab_tile=P_AB // tn, ab_off=P_AB % tn),
        out_shape=(jax.ShapeDtypeStruct((t, P_WIDTH), BF16), jax.ShapeDtypeStruct((t, LANES), F32),
                   *[jax.ShapeDtypeStruct(w.shape, BF16) for w in weights]),
        grid_spec=pltpu.PrefetchScalarGridSpec(
            num_scalar_prefetch=1,
            grid=(n_row, n_col),
            in_specs=[
                pl.BlockSpec((tm, d), lambda i, j, starts: (i, 0)),
                pl.BlockSpec((1, d), lambda i, j, starts: (0, 0)),
                pl.BlockSpec((pl.Element(tn), pl.Element(d)),
                             lambda i, j, starts: (pl.multiple_of(starts[j], 8), 0)),
                *w_specs,
            ],
            out_specs=(pl.BlockSpec((tm, tn), lambda i, j, starts: (i, j)),
                       pl.BlockSpec((tm, LANES), lambda i, j, starts: (i, 0)),
                       *w_specs),
            scratch_shapes=[pltpu.VMEM((tm, d), BF16)],
        ),
        compiler_params=pltpu.CompilerParams(
            dimension_semantics=("arbitrary", "arbitrary"), vmem_limit_bytes=VMEM_LIMIT),
        name="in_proj",
    )(jnp.asarray(IN_PROJ_TILE_STARTS, jnp.int32), x, g, wt, *weights)


def _mem_kv_kernel(m_ref, g_ref, w_ref, o_ref):
    o_ref[...] = _dot(_rms(m_ref[...], g_ref[...]).astype(BF16), w_ref[...]).astype(o_ref.dtype)


def _mem_kv(mem, g, w):
    t, d = mem.shape
    n = w.shape[1]
    return pl.pallas_call(
        _mem_kv_kernel,
        out_shape=jax.ShapeDtypeStruct((t, n), BF16),
        grid=(t // N_MEM,),
        in_specs=[
            pl.BlockSpec((N_MEM, d), lambda i: (i, 0)),
            pl.BlockSpec((1, d), lambda i: (0, 0)),
            pl.BlockSpec((d, n), lambda i: (0, 0)),
        ],
        out_specs=pl.BlockSpec((N_MEM, n), lambda i: (i, 0)),
        compiler_params=pltpu.CompilerParams(
            dimension_semantics=("arbitrary",), vmem_limit_bytes=VMEM_LIMIT),
        name="mem_kv",
    )(mem, g, w)


def _swa_block(sinks_ref, q_ref, rows, k, v, first, o_ref):
    w = SWA_WINDOW
    hd = SWA_HEAD_DIM
    group = SWA_Q_HEADS // SWA_KV_HEADS
    ones = jnp.ones((2 * w, hd), BF16)
    qi = lax.broadcasted_iota(jnp.int32, (w, 2 * w), 0)
    kj = lax.broadcasted_iota(jnp.int32, (w, 2 * w), 1)
    valid = (kj > qi) & (kj <= qi + w)
    if first is not False:
        valid = valid & (kj >= jnp.where(first, w, 0))
    heads = range(SWA_Q_HEADS)
    kh = [k[:, hk * hd:(hk + 1) * hd] for hk in range(SWA_KV_HEADS)]
    vh = [jnp.concatenate([v[:, hk * hd:(hk + 1) * hd], ones], axis=1) for hk in range(SWA_KV_HEADS)]
    s = [_dot_nt(q_ref[rows, h * hd:(h + 1) * hd], kh[h // group]) for h in heads]
    s = [jnp.where(valid, s[h], -jnp.inf) for h in heads]
    m = [jnp.maximum(jnp.max(s[h], axis=-1, keepdims=True), sinks_ref[h]) for h in heads]
    p = [jnp.exp(s[h] - m[h]).astype(BF16) for h in heads]
    ov = [_dot(p[h], vh[h // group]) for h in heads]
    for h in heads:
        o = ov[h][:, :hd] / (ov[h][:, hd:hd + 1] + jnp.exp(sinks_ref[h] - m[h]))
        o_ref[rows, h * hd:(h + 1) * hd] = o.astype(o_ref.dtype)


XA_SUB = 128


def _xattn_rows(q_ref, mk_ref, mv_ref, o_ref, col0):
    hd = XA_HEAD_DIM
    ones = jnp.ones((N_MEM, hd), BF16)
    units = [(slice(r, r + XA_SUB), h) for r in range(0, q_ref.shape[0], XA_SUB) for h in range(XA_HEADS)]
    mv = [jnp.concatenate([mv_ref[:, h * hd:(h + 1) * hd], ones], axis=1) for h in range(XA_HEADS)]
    s = [_dot_nt(q_ref[rs, h * hd:(h + 1) * hd], mk_ref[:, h * hd:(h + 1) * hd]) * (hd ** -0.5)
         for rs, h in units]
    e = [jnp.exp(si - jnp.max(si, axis=-1, keepdims=True)).astype(BF16) for si in s]
    oe = [_dot(ei, mv[h]) for ei, (_, h) in zip(e, units)]
    for (rs, h), oi in zip(units, oe):
        o_ref[rs, col0 + h * hd:col0 + (h + 1) * hd] = (oi[:, :hd] / oi[:, hd:hd + 1]).astype(o_ref.dtype)


GDN_TS = 256
GDN_QKV_BLOCK = 256
GDN_QKV_BLOCKS = 3 * GDN_W // GDN_QKV_BLOCK
assert P_QB % GDN_QKV_BLOCK == 0
CONV_PAD = 8


def _gdn_kernel(*refs):
    qkv_refs = refs[:GDN_QKV_BLOCKS]
    (z_ref, ab_ref, cw_ref, alog_ref, dtb_ref, nw_ref, y_ref,
     state_ref, carry_ref, xpad_ref, cv_ref) = refs[GDN_QKV_BLOCKS:]
    ts = GDN_TS
    c = GDN_CHUNK
    dh = GDN_HEAD_DIM
    gw = GDN_W
    heads = range(GDN_HEADS)
    chunks = [slice(i * c, (i + 1) * c) for i in range(ts // c)]

    @pl.when(pl.program_id(1) == 0)
    def _():
        state_ref[...] = jnp.zeros_like(state_ref)
        carry_ref[...] = jnp.zeros_like(carry_ref)

    xpad_ref[0:CONV_PAD, :] = carry_ref[...]
    for i, r in enumerate(qkv_refs):
        xpad_ref[CONV_PAD:, i * GDN_QKV_BLOCK:(i + 1) * GDN_QKV_BLOCK] = r[...].astype(F32)
    carry_ref[...] = xpad_ref[ts:ts + CONV_PAD, :]
    for cb in range(3 * gw // LANES):
        cs = slice(cb * LANES, (cb + 1) * LANES)
        acc = None
        for i in range(GDN_CONV):
            off = CONV_PAD - (GDN_CONV - 1) + i
            term = cw_ref[i:i + 1, cs] * xpad_ref[off:off + ts, cs]
            acc = term if acc is None else acc + term
        cv_ref[:, cs] = acc * jax.nn.sigmoid(acc)

    ab = ab_ref[...]
    g = -jnp.exp(alog_ref[...]) * jax.nn.softplus(ab + dtb_ref[...])
    beta_all = jax.nn.sigmoid(ab)
    ri = lax.broadcasted_iota(jnp.int32, (ts, ts), 0)
    ci = lax.broadcasted_iota(jnp.int32, (ts, ts), 1)
    same_chunk = (ri // c) == (ci // c)
    causal = same_chunk & (ri >= ci)
    strict = same_chunk & (ri > ci)
    gcum = _dot_f32(causal.astype(F32), g)
    gcum_t = gcum.T
    g_last = jnp.concatenate(
        [jnp.broadcast_to(gcum[rc.stop - 1:rc.stop], (c, LANES)) for rc in chunks], axis=0)
    k_scale = jnp.exp(g_last - gcum)
    eg_all = jnp.exp(gcum)

    q, k, v, gc, beta = [], [], [], [], []
    for h in heads:
        qh = cv_ref[:, h * dh:(h + 1) * dh]
        kh = cv_ref[:, gw + h * dh:gw + (h + 1) * dh]
        q.append(qh * lax.rsqrt(jnp.sum(qh * qh, axis=-1, keepdims=True) + L2_EPS) * (dh ** -0.5))
        k.append(kh * lax.rsqrt(jnp.sum(kh * kh, axis=-1, keepdims=True) + L2_EPS))
        v.append(cv_ref[:, 2 * gw + h * dh:2 * gw + (h + 1) * dh])
        gc.append(gcum[:, h:h + 1])
        beta.append(beta_all[:, GDN_HEADS + h:GDN_HEADS + h + 1])
    qkk = [_dot_nt(jnp.concatenate([q[h], k[h]], axis=0).astype(BF16), k[h].astype(BF16)) for h in heads]
    decay = [jnp.exp(jnp.where(causal, gc[h] - gcum_t[h:h + 1, :], -jnp.inf)) for h in heads]
    qk = [qkk[h][:ts] * decay[h] for h in heads]
    m = [-jnp.where(strict, beta[h] * qkk[h][ts:] * decay[h], 0.0) for h in heads]
    sol = [jnp.concatenate([v[h] * beta[h], k[h] * (beta[h] * eg_all[:, h:h + 1])], axis=-1) for h in heads]
    n_rounds = c.bit_length() - 1
    for r in range(n_rounds):
        last = r + 1 == n_rounds
        mb = [m[h].astype(BF16) for h in heads]
        rhs = [sol[h].astype(BF16) if last else jnp.concatenate([sol[h], m[h]], axis=-1).astype(BF16)
               for h in heads]
        prod = [_dot(mb[h], rhs[h]) for h in heads]
        sol = [sol[h] + prod[h][:, :2 * dh] for h in heads]
        if not last:
            m = [prod[h][:, 2 * dh:] for h in heads]
    k_dec = [k[h] * k_scale[:, h:h + 1] for h in heads]
    q_dec = [q[h] * eg_all[:, h:h + 1] for h in heads]

    x = {}
    for ic, rc in enumerate(chunks):
        for h in heads:
            lhs = jnp.concatenate([k_dec[h][rc].T, qk[h][rc, rc]], axis=0).astype(BF16)
            x[ic, h] = _dot(lhs, sol[h][rc].astype(BF16))

    for ic, rc in enumerate(chunks):
        for h in heads:
            xs = x[ic, h]
            lhs = jnp.concatenate([-xs[:dh, dh:], q_dec[h][rc] - xs[dh:, dh:]], axis=0).astype(BF16)
            s = state_ref[h]
            y = _dot(lhs, s.astype(BF16))
            state_ref[h] = s * eg_all[rc.stop - 1:rc.stop, h:h + 1] + y[:dh] + xs[:dh, :dh]
            o = _rms(y[dh:] + xs[dh:, :dh], nw_ref[...])
            hs = slice(h * dh, (h + 1) * dh)
            z = z_ref[rc, hs].astype(F32)
            y_ref[rc, hs] = (o * (z * jax.nn.sigmoid(z))).astype(y_ref.dtype)


def _gdn(p, ab, conv_w, alog_row, dtb_row, norm_w, batch, seq):
    t = p.shape[0]
    ts = GDN_TS
    nj = seq // ts
    gw = GDN_W

    def rows(col_blk):
        return lambda b, j: (b * nj + j, col_blk)

    return pl.pallas_call(
        _gdn_kernel,
        out_shape=jax.ShapeDtypeStruct((t, gw), BF16),
        grid=(batch, nj),
        in_specs=[
            *[pl.BlockSpec((ts, GDN_QKV_BLOCK), rows(P_QB // GDN_QKV_BLOCK + i)) for i in range(GDN_QKV_BLOCKS)],
            pl.BlockSpec((ts, gw), rows(P_Z // gw)),
            pl.BlockSpec((ts, LANES), rows(0)),
            pl.BlockSpec((GDN_CONV, 3 * gw), lambda b, j: (0, 0)),
            pl.BlockSpec((1, LANES), lambda b, j: (0, 0)),
            pl.BlockSpec((1, LANES), lambda b, j: (0, 0)),
            pl.BlockSpec((1, GDN_HEAD_DIM), lambda b, j: (0, 0)),
        ],
        out_specs=pl.BlockSpec((ts, gw), rows(0)),
        scratch_shapes=[
            pltpu.VMEM((GDN_HEADS, GDN_HEAD_DIM, GDN_HEAD_DIM), F32),
            pltpu.VMEM((CONV_PAD, 3 * gw), F32),
            pltpu.VMEM((ts + CONV_PAD, 3 * gw), F32),
            pltpu.VMEM((ts, 3 * gw), F32),
        ],
        compiler_params=pltpu.CompilerParams(dimension_semantics=("arbitrary", "arbitrary")),
        name="gdn",
    )(*([p] * GDN_QKV_BLOCKS), p, ab, conv_w, alog_row, dtb_row, norm_w)


def _mix_kernel(sinks_ref, q_ref, kv_ref, kvp_ref, qc_ref, mk_ref, mv_ref, x_ref, ga_ref, gb_ref, gc_ref, yb_ref,
                wa_ref, wb_ref, wc_ref, wo_ref, h_ref, y0_ref, y1_ref, *, tiles_per_seq, n_tiles):
    i = pl.program_id(0)
    w = SWA_WINDOW
    hd = SWA_HEAD_DIM

    @pl.when(i == 0)
    def _():
        y0_ref[...] = jnp.zeros_like(y0_ref)

    def step(yprev_ref, ynext_ref):
        first = (jnp.minimum(i, n_tiles - 1) % tiles_per_seq) == 0
        kvw = SWA_KV_W
        k_all = (jnp.concatenate([kvp_ref[:, :kvw], kv_ref[:, :kvw]], axis=0).astype(F32)
                 * (hd ** -0.5)).astype(BF16)
        v_all = jnp.concatenate([kvp_ref[:, kvw:], kv_ref[:, kvw:]], axis=0)
        for b in range(q_ref.shape[0] // w):
            _swa_block(sinks_ref, q_ref, slice(b * w, (b + 1) * w), k_all[b * w:(b + 2) * w],
                       v_all[b * w:(b + 2) * w], first if b == 0 else False, ynext_ref)
        _xattn_rows(qc_ref, mk_ref, mv_ref, ynext_ref, SWA_Q_W)

        merged = jax.nn.sigmoid(ga_ref[...].astype(F32)) * _dot(yprev_ref[:, :SWA_Q_W], wa_ref[...])
        merged += jax.nn.sigmoid(gb_ref[...].astype(F32)) * _dot(yb_ref[...], wb_ref[...])
        merged += jax.nn.sigmoid(gc_ref[...].astype(F32)) * _dot(yprev_ref[:, SWA_Q_W:], wc_ref[...])
        h_ref[...] = x_ref[...] + _dot(merged.astype(BF16), wo_ref[...])

    @pl.when(i % 2 == 0)
    def _():
        step(y0_ref, y1_ref)

    @pl.when(i % 2 == 1)
    def _():
        step(y1_ref, y0_ref)


def _mix(x, p, mkv, yb, sinks, wa, wb, wc, wo, seq, *, tm=256):
    t, d = x.shape
    w = SWA_WINDOW
    n_tiles = t // tm
    tiles_per_seq = seq // tm
    kv_w = 2 * SWA_KV_W
    assert P_VA == P_KA + SWA_KV_W and P_KA % kv_w == 0

    def att(i):
        return jnp.minimum(i, n_tiles - 1)

    def mm(i):
        return jnp.maximum(i - 1, 0)

    def resident(shape):
        return pl.BlockSpec(shape, lambda i: (0, 0), pipeline_mode=pl.Buffered(1))

    return pl.pallas_call(
        functools.partial(_mix_kernel, tiles_per_seq=tiles_per_seq, n_tiles=n_tiles),
        out_shape=jax.ShapeDtypeStruct((t, d), F32),
        grid=(n_tiles + 1,),
        in_specs=[
            pl.BlockSpec(memory_space=pltpu.SMEM),
            pl.BlockSpec((tm, SWA_Q_W), lambda i: (att(i), P_QA // SWA_Q_W)),
            pl.BlockSpec((tm, kv_w), lambda i: (att(i), P_KA // kv_w)),
            pl.BlockSpec((w, kv_w), lambda i: (jnp.maximum(att(i) * (tm // w) - 1, 0), P_KA // kv_w)),
            pl.BlockSpec((tm, XA_W), lambda i: (att(i), P_QC // XA_W)),
            pl.BlockSpec((N_MEM, XA_W), lambda i: (att(i) // tiles_per_seq, 0)),
            pl.BlockSpec((N_MEM, XA_W), lambda i: (att(i) // tiles_per_seq, 1)),
            pl.BlockSpec((tm, d), lambda i: (mm(i), 0)),
            pl.BlockSpec((tm, d), lambda i: (mm(i), P_GATE // d)),
            pl.BlockSpec((tm, d), lambda i: (mm(i), P_GATE // d + 1)),
            pl.BlockSpec((tm, d), lambda i: (mm(i), P_GATE // d + 2)),
            pl.BlockSpec((tm, GDN_W), lambda i: (mm(i), 0)),
            resident((SWA_Q_W, d)),
            resident((GDN_W, d)),
            resident((XA_W, d)),
            resident((d, d)),
        ],
        out_specs=pl.BlockSpec((tm, d), lambda i: (mm(i), 0)),
        scratch_shapes=[pltpu.VMEM((tm, SWA_Q_W + XA_W), BF16), pltpu.VMEM((tm, SWA_Q_W + XA_W), BF16)],
        compiler_params=pltpu.CompilerParams(
            dimension_semantics=("arbitrary",), vmem_limit_bytes=VMEM_LIMIT),
        name="mix",
    )(sinks, p, p, p, p, mkv, mkv, x, p, p, p, yb, wa, wb, wc, wo)


def _mlp_kernel(h_ref, g_ref, w1_ref, w2_ref, gf_ref, o_ref, n_ref, acc_ref, *, final_norm):
    j = pl.program_id(1)

    @pl.when(j == 0)
    def _():
        h = h_ref[...]
        n_ref[...] = _rms(h, g_ref[...]).astype(BF16)
        acc_ref[...] = h

    u = _dot(n_ref[...], w1_ref[...])
    acc_ref[...] += _dot(jnp.square(jnp.maximum(u, 0.0)).astype(BF16), w2_ref[...])

    @pl.when(j == pl.num_programs(1) - 1)
    def _():
        h = acc_ref[...]
        o_ref[...] = _rms(h, gf_ref[...]) if final_norm else h


def _mlp(h, g, w1, w2, g_final, *, final_norm, tm=512, tf=1024):
    t, d = h.shape
    f = w1.shape[1]
    return pl.pallas_call(
        functools.partial(_mlp_kernel, final_norm=final_norm),
        out_shape=jax.ShapeDtypeStruct((t, d), F32),
        grid=(t // tm, f // tf),
        in_specs=[
            pl.BlockSpec((tm, d), lambda i, j: (i, 0)),
            pl.BlockSpec((1, d), lambda i, j: (0, 0)),
            pl.BlockSpec((d, tf), lambda i, j: (0, j)),
            pl.BlockSpec((tf, d), lambda i, j: (j, 0)),
            pl.BlockSpec((1, d), lambda i, j: (0, 0)),
        ],
        out_specs=pl.BlockSpec((tm, d), lambda i, j: (i, 0)),
        scratch_shapes=[pltpu.VMEM((tm, d), BF16), pltpu.VMEM((tm, d), F32)],
        compiler_params=pltpu.CompilerParams(
            dimension_semantics=("arbitrary", "arbitrary"), vmem_limit_bytes=VMEM_LIMIT),
        name="mlp",
    )(h, g, w1, w2, g_final)


def _lane_row(v):
    return jnp.zeros((1, LANES), F32).at[0, :v.shape[0]].set(v.astype(F32))


def kernel(x, mem, g_mix, w_in, sinks, conv_w, a_log, dt_bias, gdn_norm_w, g_mem, w_mem_kv, w_swa_up,
           w_gdn_up, w_xa_up, w_out, g_mlp, w_mlp_in, w_mlp_out, g_final):
    batch, seq, d = x.shape
    depth = w_in.shape[0]
    h = x.reshape(batch * seq, d)
    mem2 = mem.reshape(batch * N_MEM, d)
    for l in range(depth):
        p, ab, w1, w2, wo, wa, wb, wc, wm = _in_proj(
            h, g_mix[l][None], jnp.swapaxes(w_in[l], 0, 1).astype(BF16), w_mlp_in[l], w_mlp_out[l], w_out[l],
            w_swa_up[l], w_gdn_up[l], w_xa_up[l], w_mem_kv[l])
        mkv = _mem_kv(mem2, g_mem[l][None], wm)
        yb = _gdn(p, ab, conv_w[l], _lane_row(a_log[l]), _lane_row(dt_bias[l]), gdn_norm_w[l][None],
                  batch, seq)
        h = _mix(h, p, mkv, yb, sinks[l], wa, wb, wc, wo, seq)
        h = _mlp(h, g_mlp[l][None], w1, w2, g_final[None], final_norm=(l == depth - 1))
    return h.reshape(batch, seq, d)
```

```python
import functools

import jax
import jax.numpy as jnp
from jax import lax
from jax.experimental import pallas as pl
from jax.experimental.pallas import tpu as pltpu

F32 = jnp.float32
BF16 = jnp.bfloat16

D_MODEL = 2048
SWA_Q_HEADS = 16
SWA_KV_HEADS = 2
SWA_HEAD_DIM = 64
SWA_WINDOW = 128
GDN_HEADS = 4
GDN_HEAD_DIM = 128
GDN_CONV = 4
GDN_CHUNK = 64
N_MEM = 256
XA_HEADS = 4
XA_HEAD_DIM = 128
D_FF = 4 * D_MODEL
RMS_EPS = 1e-6
L2_EPS = 1e-6

SWA_Q_W = SWA_Q_HEADS * SWA_HEAD_DIM
SWA_KV_W = SWA_KV_HEADS * SWA_HEAD_DIM
GDN_W = GDN_HEADS * GDN_HEAD_DIM
XA_W = XA_HEADS * XA_HEAD_DIM

LANES = 128
SUBLANES = 8
BF16_SUBLANES = 16

AB_W = 2 * GDN_HEADS

R_QA = 0
R_KA = R_QA + SWA_Q_W
R_VA = R_KA + SWA_KV_W
R_QB = R_VA + SWA_KV_W
R_AB = R_QB + 3 * GDN_W
R_Z = R_AB + AB_W
R_QC = R_Z + GDN_W
R_GATE = R_QC + XA_W
R_END = R_GATE + 3 * D_MODEL

IN_PROJ_TN = 1024
IN_PROJ_TILE_STARTS = (
    tuple(R_GATE + k * IN_PROJ_TN for k in range(3 * D_MODEL // IN_PROJ_TN))
    + tuple(range(0, R_Z, IN_PROJ_TN))
    + tuple(R_Z + k * IN_PROJ_TN for k in range((R_GATE - R_Z) // IN_PROJ_TN)))
P_WIDTH = len(IN_PROJ_TILE_STARTS) * IN_PROJ_TN
P_GATE = 0
P_FRONT = 3 * D_MODEL
P_QA = P_FRONT + R_QA
P_KA = P_FRONT + R_KA
P_VA = P_FRONT + R_VA
P_QB = P_FRONT + R_QB
P_AB = P_FRONT + R_AB
P_Z = P_FRONT + -(-R_Z // IN_PROJ_TN) * IN_PROJ_TN
P_QC = P_Z + GDN_W
assert (R_GATE - R_Z) % IN_PROJ_TN == 0 and (3 * D_MODEL) % IN_PROJ_TN == 0
assert all(s % SUBLANES == 0 for s in IN_PROJ_TILE_STARTS) and P_AB % LANES == 0 and P_QC + XA_W == P_WIDTH

VMEM_LIMIT = 56 * 1024 * 1024


def _rms(x, g):
    return x * lax.rsqrt(jnp.mean(x * x, axis=-1, keepdims=True) + RMS_EPS) * g


def _dot(a, b):
    return jnp.dot(a, b, preferred_element_type=F32)


def _dot_nt(a, b):
    return lax.dot_general(a, b, (((1,), (1,)), ((), ())), preferred_element_type=F32)


def _dot_f32(a, b):
    return jnp.dot(a, b, preferred_element_type=F32, precision=lax.Precision.HIGHEST)


CAST_J = 8


def _in_proj_kernel(starts_ref, x_ref, g_ref, wt_ref, w1_ref, w2_ref, wo_ref, wa_ref, wb_ref, wc_ref, wm_ref,
                    o_ref, ab_ref, w1b_ref, w2b_ref, wob_ref, wab_ref, wbb_ref, wcb_ref, wmb_ref, n_ref, *,
                    ab_tile, ab_off):
    del starts_ref
    j = pl.program_id(1)

    @pl.when(j == 0)
    def _():
        n_ref[...] = _rms(x_ref[...], g_ref[...]).astype(BF16)

    acc = _dot_nt(n_ref[...], wt_ref[...].astype(BF16))
    o_ref[...] = acc.astype(o_ref.dtype)

    @pl.when(j < CAST_J)
    def _():
        w1b_ref[...] = w1_ref[...].astype(BF16)
        w2b_ref[...] = w2_ref[...].astype(BF16)

    @pl.when(j == CAST_J)
    def _():
        wob_ref[...] = wo_ref[...].astype(BF16)
        wab_ref[...] = wa_ref[...].astype(BF16)

    @pl.when(j == CAST_J + 1)
    def _():
        wbb_ref[...] = wb_ref[...].astype(BF16)
        wcb_ref[...] = wc_ref[...].astype(BF16)
        wmb_ref[...] = wm_ref[...].astype(BF16)

    @pl.when(j == ab_tile)
    def _():
        ab_ref[...] = acc[:, ab_off:ab_off + LANES]


def _in_proj(x, g, wt, w1, w2, wo, wa, wb, wc, wm, *, tm=1024, tn=IN_PROJ_TN):
    t, d = x.shape
    n_row, n_col = t // tm, P_WIDTH // tn
    assert n_col == CAST_J + 2

    def per_step(w):
        r, c = w.shape
        assert r % (n_row * BF16_SUBLANES) == 0 and c % (CAST_J * LANES) == 0
        return pl.BlockSpec((r // n_row, c // CAST_J), lambda i, j, starts: (i, jnp.minimum(j, CAST_J - 1)))

    def per_row_tile(w):
        r, c = w.shape
        assert r % (n_row * BF16_SUBLANES) == 0
        return pl.BlockSpec((r // n_row, c), lambda i, j, starts: (i, 0))

    weights = (w1, w2, wo, wa, wb, wc, wm)
    w_specs = [per_step(w1), per_step(w2)] + [per_row_tile(w) for w in weights[2:]]
    return pl.pallas_call(
        functools.partial(_in_proj_kernel, ab_tile=P_AB // tn, ab_off=P_AB % tn),
        out_shape=(jax.ShapeDtypeStruct((t, P_WIDTH), BF16), jax.ShapeDtypeStruct((t, LANES), F32),
                   *[jax.ShapeDtypeStruct(w.shape, BF16) for w in weights]),
        grid_spec=pltpu.PrefetchScalarGridSpec(
            num_scalar_prefetch=1,
            grid=(n_row, n_col),
            in_specs=[
                pl.BlockSpec((tm, d), lambda i, j, starts: (i, 0)),
                pl.BlockSpec((1, d), lambda i, j, starts: (0, 0)),
                pl.BlockSpec((pl.Element(tn), pl.Element(d)),
                             lambda i, j, starts: (pl.multiple_of(starts[j], SUBLANES), 0)),
                *w_specs,
            ],
            out_specs=(pl.BlockSpec((tm, tn), lambda i, j, starts: (i, j)),
                       pl.BlockSpec((tm, LANES), lambda i, j, starts: (i, 0)),
                       *w_specs),
            scratch_shapes=[pltpu.VMEM((tm, d), BF16)],
        ),
        compiler_params=pltpu.CompilerParams(
            dimension_semantics=("arbitrary", "arbitrary"), vmem_limit_bytes=VMEM_LIMIT),
        name="in_proj",
    )(jnp.asarray(IN_PROJ_TILE_STARTS, jnp.int32), x, g, wt, *weights)


def _mem_kv_kernel(m_ref, g_ref, w_ref, o_ref):
    o_ref[...] = _dot(_rms(m_ref[...], g_ref[...]).astype(BF16), w_ref[...]).astype(o_ref.dtype)


def _mem_kv(mem, g, w):
    t, d = mem.shape
    n = w.shape[1]
    return pl.pallas_call(
        _mem_kv_kernel,
        out_shape=jax.ShapeDtypeStruct((t, n), BF16),
        grid=(t // N_MEM,),
        in_specs=[
            pl.BlockSpec((N_MEM, d), lambda i: (i, 0)),
            pl.BlockSpec((1, d), lambda i: (0, 0)),
            pl.BlockSpec((d, n), lambda i: (0, 0)),
        ],
        out_specs=pl.BlockSpec((N_MEM, n), lambda i: (i, 0)),
        compiler_params=pltpu.CompilerParams(
            dimension_semantics=("arbitrary",), vmem_limit_bytes=VMEM_LIMIT),
        name="mem_kv",
    )(mem, g, w)


def _swa_block(sinks_ref, q_ref, rows, k, v, first, o_ref):
    w = SWA_WINDOW
    hd = SWA_HEAD_DIM
    group = SWA_Q_HEADS // SWA_KV_HEADS
    ones = jnp.ones((2 * w, hd), BF16)
    qi = lax.broadcasted_iota(jnp.int32, (w, 2 * w), 0)
    kj = lax.broadcasted_iota(jnp.int32, (w, 2 * w), 1)
    valid = (kj > qi) & (kj <= qi + w)
    if first is not False:
        valid = valid & (kj >= jnp.where(first, w, 0))
    heads = range(SWA_Q_HEADS)
    kh = [k[:, hk * hd:(hk + 1) * hd] for hk in range(SWA_KV_HEADS)]
    vh = [jnp.concatenate([v[:, hk * hd:(hk + 1) * hd], ones], axis=1) for hk in range(SWA_KV_HEADS)]
    s = [_dot_nt(q_ref[rows, h * hd:(h + 1) * hd], kh[h // group]) for h in heads]
    s = [jnp.where(valid, s[h], -jnp.inf) for h in heads]
    m = [jnp.maximum(jnp.max(s[h], axis=-1, keepdims=True), sinks_ref[h]) for h in heads]
    p = [jnp.exp(s[h] - m[h]).astype(BF16) for h in heads]
    ov = [_dot(p[h], vh[h // group]) for h in heads]
    for h in heads:
        o = ov[h][:, :hd] / (ov[h][:, hd:hd + 1] + jnp.exp(sinks_ref[h] - m[h]))
        o_ref[rows, h * hd:(h + 1) * hd] = o.astype(o_ref.dtype)


XA_SUB = 128


def _xattn_rows(q_ref, mk_ref, mv_ref, o_ref, col0):
    hd = XA_HEAD_DIM
    ones = jnp.ones((N_MEM, hd), BF16)
    units = [(slice(r, r + XA_SUB), h) for r in range(0, q_ref.shape[0], XA_SUB) for h in range(XA_HEADS)]
    mv = [jnp.concatenate([mv_ref[:, h * hd:(h + 1) * hd], ones], axis=1) for h in range(XA_HEADS)]
    s = [_dot_nt(q_ref[rs, h * hd:(h + 1) * hd], mk_ref[:, h * hd:(h + 1) * hd]) * (hd ** -0.5)
         for rs, h in units]
    e = [jnp.exp(si - jnp.max(si, axis=-1, keepdims=True)).astype(BF16) for si in s]
    oe = [_dot(ei, mv[h]) for ei, (_, h) in zip(e, units)]
    for (rs, h), oi in zip(units, oe):
        o_ref[rs, col0 + h * hd:col0 + (h + 1) * hd] = (oi[:, :hd] / oi[:, hd:hd + 1]).astype(o_ref.dtype)


GDN_TS = 256
GDN_QKV_BLOCK = 256
GDN_QKV_BLOCKS = 3 * GDN_W // GDN_QKV_BLOCK
assert P_QB % GDN_QKV_BLOCK == 0
CONV_PAD = SUBLANES


def _gdn_kernel(*refs):
    qkv_refs = refs[:GDN_QKV_BLOCKS]
    (z_ref, ab_ref, cw_ref, alog_ref, dtb_ref, nw_ref, y_ref,
     state_ref, carry_ref, xpad_ref, cv_ref) = refs[GDN_QKV_BLOCKS:]
    ts = GDN_TS
    c = GDN_CHUNK
    dh = GDN_HEAD_DIM
    gw = GDN_W
    heads = range(GDN_HEADS)
    chunks = [slice(i * c, (i + 1) * c) for i in range(ts // c)]

    @pl.when(pl.program_id(1) == 0)
    def _():
        state_ref[...] = jnp.zeros_like(state_ref)
        carry_ref[...] = jnp.zeros_like(carry_ref)

    xpad_ref[0:CONV_PAD, :] = carry_ref[...]
    for i, r in enumerate(qkv_refs):
        xpad_ref[CONV_PAD:, i * GDN_QKV_BLOCK:(i + 1) * GDN_QKV_BLOCK] = r[...].astype(F32)
    carry_ref[...] = xpad_ref[ts:ts + CONV_PAD, :]
    for cb in range(3 * gw // LANES):
        cs = slice(cb * LANES, (cb + 1) * LANES)
        acc = None
        for i in range(GDN_CONV):
            off = CONV_PAD - (GDN_CONV - 1) + i
            term = cw_ref[i:i + 1, cs] * xpad_ref[off:off + ts, cs]
            acc = term if acc is None else acc + term
        cv_ref[:, cs] = acc * jax.nn.sigmoid(acc)

    ab = ab_ref[...]
    g = -jnp.exp(alog_ref[...]) * jax.nn.softplus(ab + dtb_ref[...])
    beta_all = jax.nn.sigmoid(ab)
    ri = lax.broadcasted_iota(jnp.int32, (ts, ts), 0)
    ci = lax.broadcasted_iota(jnp.int32, (ts, ts), 1)
    same_chunk = (ri // c) == (ci // c)
    causal = same_chunk & (ri >= ci)
    strict = same_chunk & (ri > ci)
    gcum = _dot_f32(causal.astype(F32), g)
    gcum_t = gcum.T
    g_last = jnp.concatenate(
        [jnp.broadcast_to(gcum[rc.stop - 1:rc.stop], (c, LANES)) for rc in chunks], axis=0)
    k_scale = jnp.exp(g_last - gcum)
    eg_all = jnp.exp(gcum)

    q, k, v, gc, beta = [], [], [], [], []
    for h in heads:
        qh = cv_ref[:, h * dh:(h + 1) * dh]
        kh = cv_ref[:, gw + h * dh:gw + (h + 1) * dh]
        q.append(qh * lax.rsqrt(jnp.sum(qh * qh, axis=-1, keepdims=True) + L2_EPS) * (dh ** -0.5))
        k.append(kh * lax.rsqrt(jnp.sum(kh * kh, axis=-1, keepdims=True) + L2_EPS))
        v.append(cv_ref[:, 2 * gw + h * dh:2 * gw + (h + 1) * dh])
        gc.append(gcum[:, h:h + 1])
        beta.append(beta_all[:, GDN_HEADS + h:GDN_HEADS + h + 1])
    qkk = [_dot_nt(jnp.concatenate([q[h], k[h]], axis=0).astype(BF16), k[h].astype(BF16)) for h in heads]
    decay = [jnp.exp(jnp.where(causal, gc[h] - gcum_t[h:h + 1, :], -jnp.inf)) for h in heads]
    qk = [qkk[h][:ts] * decay[h] for h in heads]
    m = [-jnp.where(strict, beta[h] * qkk[h][ts:] * decay[h], 0.0) for h in heads]
    sol = [jnp.concatenate([v[h] * beta[h], k[h] * (beta[h] * eg_all[:, h:h + 1])], axis=-1) for h in heads]
    n_rounds = c.bit_length() - 1
    for r in range(n_rounds):
        last = r + 1 == n_rounds
        mb = [m[h].astype(BF16) for h in heads]
        rhs = [sol[h].astype(BF16) if last else jnp.concatenate([sol[h], m[h]], axis=-1).astype(BF16)
               for h in heads]
        prod = [_dot(mb[h], rhs[h]) for h in heads]
        sol = [sol[h] + prod[h][:, :2 * dh] for h in heads]
        if not last:
            m = [prod[h][:, 2 * dh:] for h in heads]
    k_dec = [k[h] * k_scale[:, h:h + 1] for h in heads]
    q_dec = [q[h] * eg_all[:, h:h + 1] for h in heads]

    x = {}
    for ic, rc in enumerate(chunks):
        for h in heads:
            lhs = jnp.concatenate([k_dec[h][rc].T, qk[h][rc, rc]], axis=0).astype(BF16)
            x[ic, h] = _dot(lhs, sol[h][rc].astype(BF16))

    for ic, rc in enumerate(chunks):
        for h in heads:
            xs = x[ic, h]
            lhs = jnp.concatenate([-xs[:dh, dh:], q_dec[h][rc] - xs[dh:, dh:]], axis=0).astype(BF16)
            s = state_ref[h]
            y = _dot(lhs, s.astype(BF16))
            state_ref[h] = s * eg_all[rc.stop - 1:rc.stop, h:h + 1] + y[:dh] + xs[:dh, :dh]
            o = _rms(y[dh:] + xs[dh:, :dh], nw_ref[...])
            hs = slice(h * dh, (h + 1) * dh)
            z = z_ref[rc, hs].astype(F32)
            y_ref[rc, hs] = (o * (z * jax.nn.sigmoid(z))).astype(y_ref.dtype)


def _gdn(p, ab, conv_w, alog_row, dtb_row, norm_w, batch, seq):
    t = p.shape[0]
    ts = GDN_TS
    nj = seq // ts
    gw = GDN_W

    def rows(col_blk):
        return lambda b, j: (b * nj + j, col_blk)

    return pl.pallas_call(
        _gdn_kernel,
        out_shape=jax.ShapeDtypeStruct((t, gw), BF16),
        grid=(batch, nj),
        in_specs=[
            *[pl.BlockSpec((ts, GDN_QKV_BLOCK), rows(P_QB // GDN_QKV_BLOCK + i)) for i in range(GDN_QKV_BLOCKS)],
            pl.BlockSpec((ts, gw), rows(P_Z // gw)),
            pl.BlockSpec((ts, LANES), rows(0)),
            pl.BlockSpec((GDN_CONV, 3 * gw), lambda b, j: (0, 0)),
            pl.BlockSpec((1, LANES), lambda b, j: (0, 0)),
            pl.BlockSpec((1, LANES), lambda b, j: (0, 0)),
            pl.BlockSpec((1, GDN_HEAD_DIM), lambda b, j: (0, 0)),
        ],
        out_specs=pl.BlockSpec((ts, gw), rows(0)),
        scratch_shapes=[
            pltpu.VMEM((GDN_HEADS, GDN_HEAD_DIM, GDN_HEAD_DIM), F32),
            pltpu.VMEM((CONV_PAD, 3 * gw), F32),
            pltpu.VMEM((ts + CONV_PAD, 3 * gw), F32),
            pltpu.VMEM((ts, 3 * gw), F32),
        ],
        compiler_params=pltpu.CompilerParams(dimension_semantics=("arbitrary", "arbitrary")),
        name="gdn",
    )(*([p] * GDN_QKV_BLOCKS), p, ab, conv_w, alog_row, dtb_row, norm_w)


def _mix_kernel(sinks_ref, q_ref, kv_ref, kvp_ref, qc_ref, mk_ref, mv_ref, x_ref, ga_ref, gb_ref, gc_ref, yb_ref,
                wa_ref, wb_ref, wc_ref, wo_ref, h_ref, y0_ref, y1_ref, *, tiles_per_seq, n_tiles):
    i = pl.program_id(0)
    w = SWA_WINDOW
    hd = SWA_HEAD_DIM

    @pl.when(i == 0)
    def _():
        y0_ref[...] = jnp.zeros_like(y0_ref)

    def step(yprev_ref, ynext_ref):
        first = (jnp.minimum(i, n_tiles - 1) % tiles_per_seq) == 0
        kvw = SWA_KV_W
        k_all = (jnp.concatenate([kvp_ref[:, :kvw], kv_ref[:, :kvw]], axis=0).astype(F32)
                 * (hd ** -0.5)).astype(BF16)
        v_all = jnp.concatenate([kvp_ref[:, kvw:], kv_ref[:, kvw:]], axis=0)
        for b in range(q_ref.shape[0] // w):
            _swa_block(sinks_ref, q_ref, slice(b * w, (b + 1) * w), k_all[b * w:(b + 2) * w],
                       v_all[b * w:(b + 2) * w], first if b == 0 else False, ynext_ref)
        _xattn_rows(qc_ref, mk_ref, mv_ref, ynext_ref, SWA_Q_W)

        merged = jax.nn.sigmoid(ga_ref[...].astype(F32)) * _dot(yprev_ref[:, :SWA_Q_W], wa_ref[...])
        merged += jax.nn.sigmoid(gb_ref[...].astype(F32)) * _dot(yb_ref[...], wb_ref[...])
        merged += jax.nn.sigmoid(gc_ref[...].astype(F32)) * _dot(yprev_ref[:, SWA_Q_W:], wc_ref[...])
        h_ref[...] = x_ref[...] + _dot(merged.astype(BF16), wo_ref[...])

    @pl.when(i % 2 == 0)
    def _():
        step(y0_ref, y1_ref)

    @pl.when(i % 2 == 1)
    def _():
        step(y1_ref, y0_ref)


def _mix(x, p, mkv, yb, sinks, wa, wb, wc, wo, seq, *, tm=256):
    t, d = x.shape
    w = SWA_WINDOW
    n_tiles = t // tm
    tiles_per_seq = seq // tm
    kv_w = 2 * SWA_KV_W
    assert P_VA == P_KA + SWA_KV_W and P_KA % kv_w == 0

    def att(i):
        return jnp.minimum(i, n_tiles - 1)

    def mm(i):
        return jnp.maximum(i - 1, 0)

    def resident(shape):
        return pl.BlockSpec(shape, lambda i: (0, 0), pipeline_mode=pl.Buffered(1))

    return pl.pallas_call(
        functools.partial(_mix_kernel, tiles_per_seq=tiles_per_seq, n_tiles=n_tiles),
        out_shape=jax.ShapeDtypeStruct((t, d), F32),
        grid=(n_tiles + 1,),
        in_specs=[
            pl.BlockSpec(memory_space=pltpu.SMEM),
            pl.BlockSpec((tm, SWA_Q_W), lambda i: (att(i), P_QA // SWA_Q_W)),
            pl.BlockSpec((tm, kv_w), lambda i: (att(i), P_KA // kv_w)),
            pl.BlockSpec((w, kv_w), lambda i: (jnp.maximum(att(i) * (tm // w) - 1, 0), P_KA // kv_w)),
            pl.BlockSpec((tm, XA_W), lambda i: (att(i), P_QC // XA_W)),
            pl.BlockSpec((N_MEM, XA_W), lambda i: (att(i) // tiles_per_seq, 0)),
            pl.BlockSpec((N_MEM, XA_W), lambda i: (att(i) // tiles_per_seq, 1)),
            pl.BlockSpec((tm, d), lambda i: (mm(i), 0)),
            pl.BlockSpec((tm, d), lambda i: (mm(i), P_GATE // d)),
            pl.BlockSpec((tm, d), lambda i: (mm(i), P_GATE // d + 1)),
            pl.BlockSpec((tm, d), lambda i: (mm(i), P_GATE // d + 2)),
            pl.BlockSpec((tm, GDN_W), lambda i: (mm(i), 0)),
            resident((SWA_Q_W, d)),
            resident((GDN_W, d)),
            resident((XA_W, d)),
            resident((d, d)),
        ],
        out_specs=pl.BlockSpec((tm, d), lambda i: (mm(i), 0)),
        scratch_shapes=[pltpu.VMEM((tm, SWA_Q_W + XA_W), BF16), pltpu.VMEM((tm, SWA_Q_W + XA_W), BF16)],
        compiler_params=pltpu.CompilerParams(
            dimension_semantics=("arbitrary",), vmem_limit_bytes=VMEM_LIMIT),
        name="mix",
    )(sinks, p, p, p, p, mkv, mkv, x, p, p, p, yb, wa, wb, wc, wo)


def _mlp_kernel(h_ref, g_ref, w1_ref, w2_ref, gf_ref, o_ref, n_ref, acc_ref, *, final_norm):
    j = pl.program_id(1)

    @pl.when(j == 0)
    def _():
        h = h_ref[...]
        n_ref[...] = _rms(h, g_ref[...]).astype(BF16)
        acc_ref[...] = h

    u = _dot(n_ref[...], w1_ref[...])
    acc_ref[...] += _dot(jnp.square(jnp.maximum(u, 0.0)).astype(BF16), w2_ref[...])

    @pl.when(j == pl.num_programs(1) - 1)
    def _():
        h = acc_ref[...]
        o_ref[...] = _rms(h, gf_ref[...]) if final_norm else h


def _mlp(h, g, w1, w2, g_final, *, final_norm, tm=512, tf=1024):
    t, d = h.shape
    f = w1.shape[1]
    return pl.pallas_call(
        functools.partial(_mlp_kernel, final_norm=final_norm),
        out_shape=jax.ShapeDtypeStruct((t, d), F32),
        grid=(t // tm, f // tf),
        in_specs=[
            pl.BlockSpec((tm, d), lambda i, j: (i, 0)),
            pl.BlockSpec((1, d), lambda i, j: (0, 0)),
            pl.BlockSpec((d, tf), lambda i, j: (0, j)),
            pl.BlockSpec((tf, d), lambda i, j: (j, 0)),
            pl.BlockSpec((1, d), lambda i, j: (0, 0)),
        ],
        out_specs=pl.BlockSpec((tm, d), lambda i, j: (i, 0)),
        scratch_shapes=[pltpu.VMEM((tm, d), BF16), pltpu.VMEM((tm, d), F32)],
        compiler_params=pltpu.CompilerParams(
            dimension_semantics=("arbitrary", "arbitrary"), vmem_limit_bytes=VMEM_LIMIT),
        name="mlp",
    )(h, g, w1, w2, g_final)


def _lane_row(v):
    return jnp.zeros((1, LANES), F32).at[0, :v.shape[0]].set(v.astype(F32))


def kernel(x, mem, g_mix, w_in, sinks, conv_w, a_log, dt_bias, gdn_norm_w, g_mem, w_mem_kv, w_swa_up,
           w_gdn_up, w_xa_up, w_out, g_mlp, w_mlp_in, w_mlp_out, g_final):
    batch, seq, d = x.shape
    depth = w_in.shape[0]
    h = x.reshape(batch * seq, d)
    mem2 = mem.reshape(batch * N_MEM, d)
    for l in range(depth):
        p, ab, w1, w2, wo, wa, wb, wc, wm = _in_proj(
            h, g_mix[l][None], jnp.swapaxes(w_in[l], 0, 1), w_mlp_in[l], w_mlp_out[l], w_out[l],
            w_swa_up[l], w_gdn_up[l], w_xa_up[l], w_mem_kv[l])
        mkv = _mem_kv(mem2, g_mem[l][None], wm)
        yb = _gdn(p, ab, conv_w[l], _lane_row(a_log[l]), _lane_row(dt_bias[l]), gdn_norm_w[l][None],
                  batch, seq)
        h = _mix(h, p, mkv, yb, sinks[l], wa, wb, wc, wo, seq)
        h = _mlp(h, g_mlp[l][None], w1, w2, g_final[None], final_norm=(l == depth - 1))
    return h.reshape(batch, seq, d)
```

```python
import functools

import jax
import jax.numpy as jnp
from jax import lax
from jax.experimental import pallas as pl
from jax.experimental.pallas import tpu as pltpu

F32 = jnp.float32
BF16 = jnp.bfloat16

D_MODEL = 2048
SWA_Q_HEADS = 16
SWA_KV_HEADS = 2
SWA_HEAD_DIM = 64
SWA_WINDOW = 128
GDN_HEADS = 4
GDN_HEAD_DIM = 128
GDN_CONV = 4
GDN_CHUNK = 64
N_MEM = 256
XA_HEADS = 4
XA_HEAD_DIM = 128
D_FF = 4 * D_MODEL
RMS_EPS = 1e-6
L2_EPS = 1e-6

SWA_Q_W = SWA_Q_HEADS * SWA_HEAD_DIM
SWA_KV_W = SWA_KV_HEADS * SWA_HEAD_DIM
GDN_W = GDN_HEADS * GDN_HEAD_DIM
XA_W = XA_HEADS * XA_HEAD_DIM

LANES = 128
SUBLANES = 8
BF16_SUBLANES = 16

AB_W = 2 * GDN_HEADS

R_QA = 0
R_KA = R_QA + SWA_Q_W
R_VA = R_KA + SWA_KV_W
R_QB = R_VA + SWA_KV_W
R_AB = R_QB + 3 * GDN_W
R_Z = R_AB + AB_W
R_QC = R_Z + GDN_W
R_GATE = R_QC + XA_W
R_END = R_GATE + 3 * D_MODEL

IN_PROJ_TN = 1024
IN_PROJ_TILE_STARTS = (
    tuple(R_GATE + k * IN_PROJ_TN for k in range(3 * D_MODEL // IN_PROJ_TN))
    + tuple(range(0, R_Z, IN_PROJ_TN))
    + tuple(R_Z + k * IN_PROJ_TN for k in range((R_GATE - R_Z) // IN_PROJ_TN)))
P_WIDTH = len(IN_PROJ_TILE_STARTS) * IN_PROJ_TN
P_GATE = 0
P_FRONT = 3 * D_MODEL
P_QA = P_FRONT + R_QA
P_KA = P_FRONT + R_KA
P_VA = P_FRONT + R_VA
P_QB = P_FRONT + R_QB
P_AB = P_FRONT + R_AB
P_Z = P_FRONT + -(-R_Z // IN_PROJ_TN) * IN_PROJ_TN
P_QC = P_Z + GDN_W
assert (R_GATE - R_Z) % IN_PROJ_TN == 0 and (3 * D_MODEL) % IN_PROJ_TN == 0
assert all(s % SUBLANES == 0 for s in IN_PROJ_TILE_STARTS) and P_AB % LANES == 0 and P_QC + XA_W == P_WIDTH

VMEM_LIMIT = 56 * 1024 * 1024


def _rms(x, g):
    return x * lax.rsqrt(jnp.mean(x * x, axis=-1, keepdims=True) + RMS_EPS) * g


def _dot(a, b):
    return jnp.dot(a, b, preferred_element_type=F32)


def _dot_nt(a, b):
    return lax.dot_general(a, b, (((1,), (1,)), ((), ())), preferred_element_type=F32)


def _dot_f32(a, b):
    return jnp.dot(a, b, preferred_element_type=F32, precision=lax.Precision.HIGHEST)


def _in_proj_kernel(starts_ref, x_ref, g_ref, wt_ref, wo_ref, wa_ref, wb_ref, wc_ref, wm_ref,
                    o_ref, ab_ref, wob_ref, wab_ref, wbb_ref, wcb_ref, wmb_ref, n_ref, *, ab_tile, ab_off):
    del starts_ref
    j = pl.program_id(1)
    last = pl.num_programs(1) - 1

    @pl.when(j == 0)
    def _():
        n_ref[...] = _rms(x_ref[...], g_ref[...]).astype(BF16)

    acc = _dot_nt(n_ref[...], wt_ref[...].astype(BF16))
    o_ref[...] = acc.astype(o_ref.dtype)

    @pl.when(j == last - 1)
    def _():
        wob_ref[...] = wo_ref[...].astype(BF16)
        wab_ref[...] = wa_ref[...].astype(BF16)

    @pl.when(j == last)
    def _():
        wbb_ref[...] = wb_ref[...].astype(BF16)
        wcb_ref[...] = wc_ref[...].astype(BF16)
        wmb_ref[...] = wm_ref[...].astype(BF16)

    @pl.when(j == ab_tile)
    def _():
        ab_ref[...] = acc[:, ab_off:ab_off + LANES]


def _in_proj(x, g, wt, wo, wa, wb, wc, wm, *, tm=1024, tn=IN_PROJ_TN):
    t, d = x.shape
    n_row, n_col = t // tm, P_WIDTH // tn

    def per_row_tile(w):
        r, c = w.shape
        assert r % (n_row * BF16_SUBLANES) == 0
        return pl.BlockSpec((r // n_row, c), lambda i, j, starts: (i, 0))

    weights = (wo, wa, wb, wc, wm)
    w_specs = [per_row_tile(w) for w in weights]
    return pl.pallas_call(
        functools.partial(_in_proj_kernel, ab_tile=P_AB // tn, ab_off=P_AB % tn),
        out_shape=(jax.ShapeDtypeStruct((t, P_WIDTH), BF16), jax.ShapeDtypeStruct((t, LANES), F32),
                   *[jax.ShapeDtypeStruct(w.shape, BF16) for w in weights]),
        grid_spec=pltpu.PrefetchScalarGridSpec(
            num_scalar_prefetch=1,
            grid=(n_row, n_col),
            in_specs=[
                pl.BlockSpec((tm, d), lambda i, j, starts: (i, 0)),
                pl.BlockSpec((1, d), lambda i, j, starts: (0, 0)),
                pl.BlockSpec((pl.Element(tn), pl.Element(d)),
                             lambda i, j, starts: (pl.multiple_of(starts[j], SUBLANES), 0)),
                *w_specs,
            ],
            out_specs=(pl.BlockSpec((tm, tn), lambda i, j, starts: (i, j)),
                       pl.BlockSpec((tm, LANES), lambda i, j, starts: (i, 0)),
                       *w_specs),
            scratch_shapes=[pltpu.VMEM((tm, d), BF16)],
        ),
        compiler_params=pltpu.CompilerParams(
            dimension_semantics=("arbitrary", "arbitrary"), vmem_limit_bytes=VMEM_LIMIT),
        name="in_proj",
    )(jnp.asarray(IN_PROJ_TILE_STARTS, jnp.int32), x, g, wt, *weights)


def _mem_kv_kernel(m_ref, g_ref, w_ref, o_ref):
    o_ref[...] = _dot(_rms(m_ref[...], g_ref[...]).astype(BF16), w_ref[...]).astype(o_ref.dtype)


def _mem_kv(mem, g, w):
    t, d = mem.shape
    n = w.shape[1]
    return pl.pallas_call(
        _mem_kv_kernel,
        out_shape=jax.ShapeDtypeStruct((t, n), BF16),
        grid=(t // N_MEM,),
        in_specs=[
            pl.BlockSpec((N_MEM, d), lambda i: (i, 0)),
            pl.BlockSpec((1, d), lambda i: (0, 0)),
            pl.BlockSpec((d, n), lambda i: (0, 0)),
        ],
        out_specs=pl.BlockSpec((N_MEM, n), lambda i: (i, 0)),
        compiler_params=pltpu.CompilerParams(
            dimension_semantics=("arbitrary",), vmem_limit_bytes=VMEM_LIMIT),
        name="mem_kv",
    )(mem, g, w)


def _swa_block(sinks_ref, q_ref, rows, k, v, first, o_ref):
    w = SWA_WINDOW
    hd = SWA_HEAD_DIM
    group = SWA_Q_HEADS // SWA_KV_HEADS
    ones = jnp.ones((2 * w, hd), BF16)
    qi = lax.broadcasted_iota(jnp.int32, (w, 2 * w), 0)
    kj = lax.broadcasted_iota(jnp.int32, (w, 2 * w), 1)
    valid = (kj > qi) & (kj <= qi + w)
    if first is not False:
        valid = valid & (kj >= jnp.where(first, w, 0))
    heads = range(SWA_Q_HEADS)
    kh = [k[:, hk * hd:(hk + 1) * hd] for hk in range(SWA_KV_HEADS)]
    vh = [jnp.concatenate([v[:, hk * hd:(hk + 1) * hd], ones], axis=1) for hk in range(SWA_KV_HEADS)]
    s = [_dot_nt(q_ref[rows, h * hd:(h + 1) * hd], kh[h // group]) for h in heads]
    s = [jnp.where(valid, s[h], -jnp.inf) for h in heads]
    m = [jnp.maximum(jnp.max(s[h], axis=-1, keepdims=True), sinks_ref[h]) for h in heads]
    p = [jnp.exp(s[h] - m[h]).astype(BF16) for h in heads]
    ov = [_dot(p[h], vh[h // group]) for h in heads]
    for h in heads:
        o = ov[h][:, :hd] / (ov[h][:, hd:hd + 1] + jnp.exp(sinks_ref[h] - m[h]))
        o_ref[rows, h * hd:(h + 1) * hd] = o.astype(o_ref.dtype)


XA_SUB = 128


def _xattn_rows(q_ref, mk_ref, mv_ref, o_ref, col0):
    hd = XA_HEAD_DIM
    ones = jnp.ones((N_MEM, hd), BF16)
    units = [(slice(r, r + XA_SUB), h) for r in range(0, q_ref.shape[0], XA_SUB) for h in range(XA_HEADS)]
    mv = [jnp.concatenate([mv_ref[:, h * hd:(h + 1) * hd], ones], axis=1) for h in range(XA_HEADS)]
    s = [_dot_nt(q_ref[rs, h * hd:(h + 1) * hd], mk_ref[:, h * hd:(h + 1) * hd]) * (hd ** -0.5)
         for rs, h in units]
    e = [jnp.exp(si - jnp.max(si, axis=-1, keepdims=True)).astype(BF16) for si in s]
    oe = [_dot(ei, mv[h]) for ei, (_, h) in zip(e, units)]
    for (rs, h), oi in zip(units, oe):
        o_ref[rs, col0 + h * hd:col0 + (h + 1) * hd] = (oi[:, :hd] / oi[:, hd:hd + 1]).astype(o_ref.dtype)


GDN_TS = 256
GDN_QKV_BLOCK = 256
GDN_QKV_BLOCKS = 3 * GDN_W // GDN_QKV_BLOCK
assert P_QB % GDN_QKV_BLOCK == 0
CONV_PAD = SUBLANES


def _gdn_kernel(*refs):
    qkv_refs = refs[:GDN_QKV_BLOCKS]
    (z_ref, ab_ref, cw_ref, alog_ref, dtb_ref, nw_ref, y_ref,
     state_ref, carry_ref, xpad_ref, cv_ref) = refs[GDN_QKV_BLOCKS:]
    ts = GDN_TS
    c = GDN_CHUNK
    dh = GDN_HEAD_DIM
    gw = GDN_W
    heads = range(GDN_HEADS)
    chunks = [slice(i * c, (i + 1) * c) for i in range(ts // c)]

    @pl.when(pl.program_id(1) == 0)
    def _():
        state_ref[...] = jnp.zeros_like(state_ref)
        carry_ref[...] = jnp.zeros_like(carry_ref)

    xpad_ref[0:CONV_PAD, :] = carry_ref[...]
    for i, r in enumerate(qkv_refs):
        xpad_ref[CONV_PAD:, i * GDN_QKV_BLOCK:(i + 1) * GDN_QKV_BLOCK] = r[...].astype(F32)
    carry_ref[...] = xpad_ref[ts:ts + CONV_PAD, :]
    for cb in range(3 * gw // LANES):
        cs = slice(cb * LANES, (cb + 1) * LANES)
        acc = None
        for i in range(GDN_CONV):
            off = CONV_PAD - (GDN_CONV - 1) + i
            term = cw_ref[i:i + 1, cs] * xpad_ref[off:off + ts, cs]
            acc = term if acc is None else acc + term
        cv_ref[:, cs] = acc * jax.nn.sigmoid(acc)

    ab = ab_ref[...]
    g = -jnp.exp(alog_ref[...]) * jax.nn.softplus(ab + dtb_ref[...])
    beta_all = jax.nn.sigmoid(ab)
    ri = lax.broadcasted_iota(jnp.int32, (ts, ts), 0)
    ci = lax.broadcasted_iota(jnp.int32, (ts, ts), 1)
    same_chunk = (ri // c) == (ci // c)
    causal = same_chunk & (ri >= ci)
    strict = same_chunk & (ri > ci)
    gcum = _dot_f32(causal.astype(F32), g)
    gcum_t = gcum.T
    g_last = jnp.concatenate(
        [jnp.broadcast_to(gcum[rc.stop - 1:rc.stop], (c, LANES)) for rc in chunks], axis=0)
    k_scale = jnp.exp(g_last - gcum)
    eg_all = jnp.exp(gcum)

    q, k, v, gc, beta = [], [], [], [], []
    for h in heads:
        qh = cv_ref[:, h * dh:(h + 1) * dh]
        kh = cv_ref[:, gw + h * dh:gw + (h + 1) * dh]
        q.append(qh * lax.rsqrt(jnp.sum(qh * qh, axis=-1, keepdims=True) + L2_EPS) * (dh ** -0.5))
        k.append(kh * lax.rsqrt(jnp.sum(kh * kh, axis=-1, keepdims=True) + L2_EPS))
        v.append(cv_ref[:, 2 * gw + h * dh:2 * gw + (h + 1) * dh])
        gc.append(gcum[:, h:h + 1])
        beta.append(beta_all[:, GDN_HEADS + h:GDN_HEADS + h + 1])
    qkk = [_dot_nt(jnp.concatenate([q[h], k[h]], axis=0).astype(BF16), k[h].astype(BF16)) for h in heads]
    decay = [jnp.exp(jnp.where(causal, gc[h] - gcum_t[h:h + 1, :], -jnp.inf)) for h in heads]
    qk = [qkk[h][:ts] * decay[h] for h in heads]
    m = [-jnp.where(strict, beta[h] * qkk[h][ts:] * decay[h], 0.0) for h in heads]
    sol = [jnp.concatenate([v[h] * beta[h], k[h] * (beta[h] * eg_all[:, h:h + 1])], axis=-1) for h in heads]
    n_rounds = c.bit_length() - 1
    for r in range(n_rounds):
        last = r + 1 == n_rounds
        mb = [m[h].astype(BF16) for h in heads]
        rhs = [sol[h].astype(BF16) if last else jnp.concatenate([sol[h], m[h]], axis=-1).astype(BF16)
               for h in heads]
        prod = [_dot(mb[h], rhs[h]) for h in heads]
        sol = [sol[h] + prod[h][:, :2 * dh] for h in heads]
        if not last:
            m = [prod[h][:, 2 * dh:] for h in heads]
    k_dec = [k[h] * k_scale[:, h:h + 1] for h in heads]
    q_dec = [q[h] * eg_all[:, h:h + 1] for h in heads]

    x = {}
    for ic, rc in enumerate(chunks):
        for h in heads:
            lhs = jnp.concatenate([k_dec[h][rc].T, qk[h][rc, rc]], axis=0).astype(BF16)
            x[ic, h] = _dot(lhs, sol[h][rc].astype(BF16))

    for ic, rc in enumerate(chunks):
        for h in heads:
            xs = x[ic, h]
            lhs = jnp.concatenate([-xs[:dh, dh:], q_dec[h][rc] - xs[dh:, dh:]], axis=0).astype(BF16)
            s = state_ref[h]
            y = _dot(lhs, s.astype(BF16))
            state_ref[h] = s * eg_all[rc.stop - 1:rc.stop, h:h + 1] + y[:dh] + xs[:dh, :dh]
            o = _rms(y[dh:] + xs[dh:, :dh], nw_ref[...])
            hs = slice(h * dh, (h + 1) * dh)
            z = z_ref[rc, hs].astype(F32)
            y_ref[rc, hs] = (o * (z * jax.nn.sigmoid(z))).astype(y_ref.dtype)


def _gdn(p, ab, conv_w, alog_row, dtb_row, norm_w, batch, seq):
    t = p.shape[0]
    ts = GDN_TS
    nj = seq // ts
    gw = GDN_W

    def rows(col_blk):
        return lambda b, j: (b * nj + j, col_blk)

    return pl.pallas_call(
        _gdn_kernel,
        out_shape=jax.ShapeDtypeStruct((t, gw), BF16),
        grid=(batch, nj),
        in_specs=[
            *[pl.BlockSpec((ts, GDN_QKV_BLOCK), rows(P_QB // GDN_QKV_BLOCK + i)) for i in range(GDN_QKV_BLOCKS)],
            pl.BlockSpec((ts, gw), rows(P_Z // gw)),
            pl.BlockSpec((ts, LANES), rows(0)),
            pl.BlockSpec((GDN_CONV, 3 * gw), lambda b, j: (0, 0)),
            pl.BlockSpec((1, LANES), lambda b, j: (0, 0)),
            pl.BlockSpec((1, LANES), lambda b, j: (0, 0)),
            pl.BlockSpec((1, GDN_HEAD_DIM), lambda b, j: (0, 0)),
        ],
        out_specs=pl.BlockSpec((ts, gw), rows(0)),
        scratch_shapes=[
            pltpu.VMEM((GDN_HEADS, GDN_HEAD_DIM, GDN_HEAD_DIM), F32),
            pltpu.VMEM((CONV_PAD, 3 * gw), F32),
            pltpu.VMEM((ts + CONV_PAD, 3 * gw), F32),
            pltpu.VMEM((ts, 3 * gw), F32),
        ],
        compiler_params=pltpu.CompilerParams(dimension_semantics=("arbitrary", "arbitrary")),
        name="gdn",
    )(*([p] * GDN_QKV_BLOCKS), p, ab, conv_w, alog_row, dtb_row, norm_w)


def _mix_kernel(sinks_ref, q_ref, kv_ref, kvp_ref, qc_ref, mk_ref, mv_ref, x_ref, ga_ref, gb_ref, gc_ref, yb_ref,
                wa_ref, wb_ref, wc_ref, wo_ref, w1_ref, w2_ref, h_ref, w1b_ref, w2b_ref, y0_ref, y1_ref, *,
                tiles_per_seq, n_tiles):
    i = pl.program_id(0)
    w = SWA_WINDOW
    hd = SWA_HEAD_DIM

    @pl.when(i == 0)
    def _():
        y0_ref[...] = jnp.zeros_like(y0_ref)

    def step(yprev_ref, ynext_ref):
        w1b_ref[...] = w1_ref[...].astype(BF16)
        w2b_ref[...] = w2_ref[...].astype(BF16)

        first = (jnp.minimum(i, n_tiles - 1) % tiles_per_seq) == 0
        kvw = SWA_KV_W
        k_all = (jnp.concatenate([kvp_ref[:, :kvw], kv_ref[:, :kvw]], axis=0).astype(F32)
                 * (hd ** -0.5)).astype(BF16)
        v_all = jnp.concatenate([kvp_ref[:, kvw:], kv_ref[:, kvw:]], axis=0)
        for b in range(q_ref.shape[0] // w):
            _swa_block(sinks_ref, q_ref, slice(b * w, (b + 1) * w), k_all[b * w:(b + 2) * w],
                       v_all[b * w:(b + 2) * w], first if b == 0 else False, ynext_ref)
        _xattn_rows(qc_ref, mk_ref, mv_ref, ynext_ref, SWA_Q_W)

        merged = jax.nn.sigmoid(ga_ref[...].astype(F32)) * _dot(yprev_ref[:, :SWA_Q_W], wa_ref[...])
        merged += jax.nn.sigmoid(gb_ref[...].astype(F32)) * _dot(yb_ref[...], wb_ref[...])
        merged += jax.nn.sigmoid(gc_ref[...].astype(F32)) * _dot(yprev_ref[:, SWA_Q_W:], wc_ref[...])
        h_ref[...] = x_ref[...] + _dot(merged.astype(BF16), wo_ref[...])

    @pl.when(i % 2 == 0)
    def _():
        step(y0_ref, y1_ref)

    @pl.when(i % 2 == 1)
    def _():
        step(y1_ref, y0_ref)


def _mix(x, p, mkv, yb, sinks, wa, wb, wc, wo, w1, w2, seq, *, tm=256):
    t, d = x.shape
    w = SWA_WINDOW
    n_tiles = t // tm
    tiles_per_seq = seq // tm
    kv_w = 2 * SWA_KV_W
    assert P_VA == P_KA + SWA_KV_W and P_KA % kv_w == 0

    def att(i):
        return jnp.minimum(i, n_tiles - 1)

    def mm(i):
        return jnp.maximum(i - 1, 0)

    def resident(shape):
        return pl.BlockSpec(shape, lambda i: (0, 0), pipeline_mode=pl.Buffered(1))

    def row_slab(wf):
        r, c = wf.shape
        assert r % (n_tiles * BF16_SUBLANES) == 0
        return pl.BlockSpec((r // n_tiles, c), lambda i: (att(i), 0))

    return pl.pallas_call(
        functools.partial(_mix_kernel, tiles_per_seq=tiles_per_seq, n_tiles=n_tiles),
        out_shape=(jax.ShapeDtypeStruct((t, d), F32), jax.ShapeDtypeStruct(w1.shape, BF16),
                   jax.ShapeDtypeStruct(w2.shape, BF16)),
        grid=(n_tiles + 1,),
        in_specs=[
            pl.BlockSpec(memory_space=pltpu.SMEM),
            pl.BlockSpec((tm, SWA_Q_W), lambda i: (att(i), P_QA // SWA_Q_W)),
            pl.BlockSpec((tm, kv_w), lambda i: (att(i), P_KA // kv_w)),
            pl.BlockSpec((w, kv_w), lambda i: (jnp.maximum(att(i) * (tm // w) - 1, 0), P_KA // kv_w)),
            pl.BlockSpec((tm, XA_W), lambda i: (att(i), P_QC // XA_W)),
            pl.BlockSpec((N_MEM, XA_W), lambda i: (att(i) // tiles_per_seq, 0)),
            pl.BlockSpec((N_MEM, XA_W), lambda i: (att(i) // tiles_per_seq, 1)),
            pl.BlockSpec((tm, d), lambda i: (mm(i), 0)),
            pl.BlockSpec((tm, d), lambda i: (mm(i), P_GATE // d)),
            pl.BlockSpec((tm, d), lambda i: (mm(i), P_GATE // d + 1)),
            pl.BlockSpec((tm, d), lambda i: (mm(i), P_GATE // d + 2)),
            pl.BlockSpec((tm, GDN_W), lambda i: (mm(i), 0)),
            resident((SWA_Q_W, d)),
            resident((GDN_W, d)),
            resident((XA_W, d)),
            resident((d, d)),
            row_slab(w1),
            row_slab(w2),
        ],
        out_specs=(pl.BlockSpec((tm, d), lambda i: (mm(i), 0)), row_slab(w1), row_slab(w2)),
        scratch_shapes=[pltpu.VMEM((tm, SWA_Q_W + XA_W), BF16), pltpu.VMEM((tm, SWA_Q_W + XA_W), BF16)],
        compiler_params=pltpu.CompilerParams(
            dimension_semantics=("arbitrary",), vmem_limit_bytes=VMEM_LIMIT),
        name="mix",
    )(sinks, p, p, p, p, mkv, mkv, x, p, p, p, yb, wa, wb, wc, wo, w1, w2)


def _mlp_kernel(h_ref, g_ref, w1_ref, w2_ref, gf_ref, o_ref, n_ref, acc_ref, *, final_norm):
    j = pl.program_id(1)

    @pl.when(j == 0)
    def _():
        h = h_ref[...]
        n_ref[...] = _rms(h, g_ref[...]).astype(BF16)
        acc_ref[...] = h

    u = _dot(n_ref[...], w1_ref[...])
    acc_ref[...] += _dot(jnp.square(jnp.maximum(u, 0.0)).astype(BF16), w2_ref[...])

    @pl.when(j == pl.num_programs(1) - 1)
    def _():
        h = acc_ref[...]
        o_ref[...] = _rms(h, gf_ref[...]) if final_norm else h


def _mlp(h, g, w1, w2, g_final, *, final_norm, tm=512, tf=1024):
    t, d = h.shape
    f = w1.shape[1]
    return pl.pallas_call(
        functools.partial(_mlp_kernel, final_norm=final_norm),
        out_shape=jax.ShapeDtypeStruct((t, d), F32),
        grid=(t // tm, f // tf),
        in_specs=[
            pl.BlockSpec((tm, d), lambda i, j: (i, 0)),
            pl.BlockSpec((1, d), lambda i, j: (0, 0)),
            pl.BlockSpec((d, tf), lambda i, j: (0, j)),
            pl.BlockSpec((tf, d), lambda i, j: (j, 0)),
            pl.BlockSpec((1, d), lambda i, j: (0, 0)),
        ],
        out_specs=pl.BlockSpec((tm, d), lambda i, j: (i, 0)),
        scratch_shapes=[pltpu.VMEM((tm, d), BF16), pltpu.VMEM((tm, d), F32)],
        compiler_params=pltpu.CompilerParams(
            dimension_semantics=("arbitrary", "arbitrary"), vmem_limit_bytes=VMEM_LIMIT),
        name="mlp",
    )(h, g, w1, w2, g_final)


def _lane_row(v):
    return jnp.zeros((1, LANES), F32).at[0, :v.shape[0]].set(v.astype(F32))


def kernel(x, mem, g_mix, w_in, sinks, conv_w, a_log, dt_bias, gdn_norm_w, g_mem, w_mem_kv, w_swa_up,
           w_gdn_up, w_xa_up, w_out, g_mlp, w_mlp_in, w_mlp_out, g_final):
    batch, seq, d = x.shape
    depth = w_in.shape[0]
    h = x.reshape(batch * seq, d)
    mem2 = mem.reshape(batch * N_MEM, d)
    for l in range(depth):
        p, ab, wo, wa, wb, wc, wm = _in_proj(
            h, g_mix[l][None], jnp.swapaxes(w_in[l], 0, 1), w_out[l], w_swa_up[l], w_gdn_up[l], w_xa_up[l],
            w_mem_kv[l])
        mkv = _mem_kv(mem2, g_mem[l][None], wm)
        yb = _gdn(p, ab, conv_w[l], _lane_row(a_log[l]), _lane_row(dt_bias[l]), gdn_norm_w[l][None],
                  batch, seq)
        h, w1, w2 = _mix(h, p, mkv, yb, sinks[l], wa, wb, wc, wo, w_mlp_in[l], w_mlp_out[l], seq)
        h = _mlp(h, g_mlp[l][None], w1, w2, g_final[None], final_norm=(l == depth - 1))
    return h.reshape(batch, seq, d)
```

```python
import functools

import jax
import jax.numpy as jnp
from jax import lax
from jax.experimental import pallas as pl
from jax.experimental.pallas import tpu as pltpu

F32 = jnp.float32
BF16 = jnp.bfloat16

D_MODEL = 2048
SWA_Q_HEADS = 16
SWA_KV_HEADS = 2
SWA_HEAD_DIM = 64
SWA_WINDOW = 128
GDN_HEADS = 4
GDN_HEAD_DIM = 128
GDN_CONV = 4
GDN_CHUNK = 64
N_MEM = 256
XA_HEADS = 4
XA_HEAD_DIM = 128
D_FF = 4 * D_MODEL
RMS_EPS = 1e-6
L2_EPS = 1e-6

SWA_Q_W = SWA_Q_HEADS * SWA_HEAD_DIM
SWA_KV_W = SWA_KV_HEADS * SWA_HEAD_DIM
GDN_W = GDN_HEADS * GDN_HEAD_DIM
XA_W = XA_HEADS * XA_HEAD_DIM

LANES = 128
SUBLANES = 8
BF16_SUBLANES = 16

AB_W = 2 * GDN_HEADS

R_QA = 0
R_KA = R_QA + SWA_Q_W
R_VA = R_KA + SWA_KV_W
R_QB = R_VA + SWA_KV_W
R_AB = R_QB + 3 * GDN_W
R_Z = R_AB + AB_W
R_QC = R_Z + GDN_W
R_GATE = R_QC + XA_W
R_END = R_GATE + 3 * D_MODEL

IN_PROJ_TN = 1024
IN_PROJ_TILE_STARTS = (
    tuple(R_GATE + k * IN_PROJ_TN for k in range(3 * D_MODEL // IN_PROJ_TN))
    + tuple(range(0, R_Z, IN_PROJ_TN))
    + tuple(R_Z + k * IN_PROJ_TN for k in range((R_GATE - R_Z) // IN_PROJ_TN)))
P_WIDTH = len(IN_PROJ_TILE_STARTS) * IN_PROJ_TN
P_GATE = 0
P_FRONT = 3 * D_MODEL
P_QA = P_FRONT + R_QA
P_KA = P_FRONT + R_KA
P_VA = P_FRONT + R_VA
P_QB = P_FRONT + R_QB
P_AB = P_FRONT + R_AB
P_Z = P_FRONT + -(-R_Z // IN_PROJ_TN) * IN_PROJ_TN
P_QC = P_Z + GDN_W
assert (R_GATE - R_Z) % IN_PROJ_TN == 0 and (3 * D_MODEL) % IN_PROJ_TN == 0
assert all(s % SUBLANES == 0 for s in IN_PROJ_TILE_STARTS) and P_AB % LANES == 0 and P_QC + XA_W == P_WIDTH

VMEM_LIMIT = 56 * 1024 * 1024


def _rms(x, g):
    return x * lax.rsqrt(jnp.mean(x * x, axis=-1, keepdims=True) + RMS_EPS) * g


def _dot(a, b):
    return jnp.dot(a, b, preferred_element_type=F32)


def _dot_nt(a, b):
    return lax.dot_general(a, b, (((1,), (1,)), ((), ())), preferred_element_type=F32)


def _dot_f32(a, b):
    return jnp.dot(a, b, preferred_element_type=F32, precision=lax.Precision.HIGHEST)


def _in_proj_kernel(starts_ref, x_ref, g_ref, wt_ref, wo_ref, wa_ref, wb_ref, wc_ref, wm_ref,
                    o_ref, ab_ref, wob_ref, wab_ref, wbb_ref, wcb_ref, wmb_ref, n_ref, *, ab_tile, ab_off):
    del starts_ref
    j = pl.program_id(1)
    last = pl.num_programs(1) - 1

    @pl.when(j == 0)
    def _():
        n_ref[...] = _rms(x_ref[...], g_ref[...]).astype(BF16)

    acc = _dot_nt(n_ref[...], wt_ref[...])
    o_ref[...] = acc.astype(o_ref.dtype)

    @pl.when(j == last - 1)
    def _():
        wob_ref[...] = wo_ref[...].astype(BF16)
        wab_ref[...] = wa_ref[...].astype(BF16)

    @pl.when(j == last)
    def _():
        wbb_ref[...] = wb_ref[...].astype(BF16)
        wcb_ref[...] = wc_ref[...].astype(BF16)
        wmb_ref[...] = wm_ref[...].astype(BF16)

    @pl.when(j == ab_tile)
    def _():
        ab_ref[...] = acc[:, ab_off:ab_off + LANES]


def _in_proj(x, g, wt, wo, wa, wb, wc, wm, *, tm=1024, tn=IN_PROJ_TN):
    t, d = x.shape
    n_row, n_col = t // tm, P_WIDTH // tn

    def per_row_tile(w):
        r, c = w.shape
        assert r % (n_row * BF16_SUBLANES) == 0
        return pl.BlockSpec((r // n_row, c), lambda i, j, starts: (i, 0))

    weights = (wo, wa, wb, wc, wm)
    w_specs = [per_row_tile(w) for w in weights]
    return pl.pallas_call(
        functools.partial(_in_proj_kernel, ab_tile=P_AB // tn, ab_off=P_AB % tn),
        out_shape=(jax.ShapeDtypeStruct((t, P_WIDTH), BF16), jax.ShapeDtypeStruct((t, LANES), F32),
                   *[jax.ShapeDtypeStruct(w.shape, BF16) for w in weights]),
        grid_spec=pltpu.PrefetchScalarGridSpec(
            num_scalar_prefetch=1,
            grid=(n_row, n_col),
            in_specs=[
                pl.BlockSpec((tm, d), lambda i, j, starts: (i, 0)),
                pl.BlockSpec((1, d), lambda i, j, starts: (0, 0)),
                pl.BlockSpec((pl.Element(tn), pl.Element(d)),
                             lambda i, j, starts: (pl.multiple_of(starts[j], SUBLANES), 0)),
                *w_specs,
            ],
            out_specs=(pl.BlockSpec((tm, tn), lambda i, j, starts: (i, j)),
                       pl.BlockSpec((tm, LANES), lambda i, j, starts: (i, 0)),
                       *w_specs),
            scratch_shapes=[pltpu.VMEM((tm, d), BF16)],
        ),
        compiler_params=pltpu.CompilerParams(
            dimension_semantics=("arbitrary", "arbitrary"), vmem_limit_bytes=VMEM_LIMIT),
        name="in_proj",
    )(jnp.asarray(IN_PROJ_TILE_STARTS, jnp.int32), x, g, wt, *weights)


def _mem_kv_kernel(m_ref, g_ref, w_ref, o_ref):
    o_ref[...] = _dot(_rms(m_ref[...], g_ref[...]).astype(BF16), w_ref[...]).astype(o_ref.dtype)


def _mem_kv(mem, g, w):
    t, d = mem.shape
    n = w.shape[1]
    return pl.pallas_call(
        _mem_kv_kernel,
        out_shape=jax.ShapeDtypeStruct((t, n), BF16),
        grid=(t // N_MEM,),
        in_specs=[
            pl.BlockSpec((N_MEM, d), lambda i: (i, 0)),
            pl.BlockSpec((1, d), lambda i: (0, 0)),
            pl.BlockSpec((d, n), lambda i: (0, 0)),
        ],
        out_specs=pl.BlockSpec((N_MEM, n), lambda i: (i, 0)),
        compiler_params=pltpu.CompilerParams(
            dimension_semantics=("arbitrary",), vmem_limit_bytes=VMEM_LIMIT),
        name="mem_kv",
    )(mem, g, w)


def _swa_block(sinks_ref, q_ref, rows, k, v, first, o_ref):
    w = SWA_WINDOW
    hd = SWA_HEAD_DIM
    group = SWA_Q_HEADS // SWA_KV_HEADS
    ones = jnp.ones((2 * w, hd), BF16)
    qi = lax.broadcasted_iota(jnp.int32, (w, 2 * w), 0)
    kj = lax.broadcasted_iota(jnp.int32, (w, 2 * w), 1)
    valid = (kj > qi) & (kj <= qi + w)
    if first is not False:
        valid = valid & (kj >= jnp.where(first, w, 0))
    heads = range(SWA_Q_HEADS)
    kh = [k[:, hk * hd:(hk + 1) * hd] for hk in range(SWA_KV_HEADS)]
    vh = [jnp.concatenate([v[:, hk * hd:(hk + 1) * hd], ones], axis=1) for hk in range(SWA_KV_HEADS)]
    s = [_dot_nt(q_ref[rows, h * hd:(h + 1) * hd], kh[h // group]) for h in heads]
    s = [jnp.where(valid, s[h], -jnp.inf) for h in heads]
    m = [jnp.maximum(jnp.max(s[h], axis=-1, keepdims=True), sinks_ref[h]) for h in heads]
    p = [jnp.exp(s[h] - m[h]).astype(BF16) for h in heads]
    ov = [_dot(p[h], vh[h // group]) for h in heads]
    for h in heads:
        o = ov[h][:, :hd] / (ov[h][:, hd:hd + 1] + jnp.exp(sinks_ref[h] - m[h]))
        o_ref[rows, h * hd:(h + 1) * hd] = o.astype(o_ref.dtype)


XA_SUB = 128


def _xattn_rows(q_ref, mk_ref, mv_ref, o_ref, col0):
    hd = XA_HEAD_DIM
    ones = jnp.ones((N_MEM, hd), BF16)
    units = [(slice(r, r + XA_SUB), h) for r in range(0, q_ref.shape[0], XA_SUB) for h in range(XA_HEADS)]
    mv = [jnp.concatenate([mv_ref[:, h * hd:(h + 1) * hd], ones], axis=1) for h in range(XA_HEADS)]
    s = [_dot_nt(q_ref[rs, h * hd:(h + 1) * hd], mk_ref[:, h * hd:(h + 1) * hd]) * (hd ** -0.5)
         for rs, h in units]
    e = [jnp.exp(si - jnp.max(si, axis=-1, keepdims=True)).astype(BF16) for si in s]
    oe = [_dot(ei, mv[h]) for ei, (_, h) in zip(e, units)]
    for (rs, h), oi in zip(units, oe):
        o_ref[rs, col0 + h * hd:col0 + (h + 1) * hd] = (oi[:, :hd] / oi[:, hd:hd + 1]).astype(o_ref.dtype)


GDN_TS = 256
GDN_QKV_BLOCK = 256
GDN_QKV_BLOCKS = 3 * GDN_W // GDN_QKV_BLOCK
assert P_QB % GDN_QKV_BLOCK == 0
CONV_PAD = SUBLANES


def _gdn_kernel(*refs):
    qkv_refs = refs[:GDN_QKV_BLOCKS]
    (z_ref, ab_ref, cw_ref, alog_ref, dtb_ref, nw_ref, y_ref,
     state_ref, carry_ref, xpad_ref, cv_ref) = refs[GDN_QKV_BLOCKS:]
    ts = GDN_TS
    c = GDN_CHUNK
    dh = GDN_HEAD_DIM
    gw = GDN_W
    heads = range(GDN_HEADS)
    chunks = [slice(i * c, (i + 1) * c) for i in range(ts // c)]

    @pl.when(pl.program_id(1) == 0)
    def _():
        state_ref[...] = jnp.zeros_like(state_ref)
        carry_ref[...] = jnp.zeros_like(carry_ref)

    xpad_ref[0:CONV_PAD, :] = carry_ref[...]
    for i, r in enumerate(qkv_refs):
        xpad_ref[CONV_PAD:, i * GDN_QKV_BLOCK:(i + 1) * GDN_QKV_BLOCK] = r[...].astype(F32)
    carry_ref[...] = xpad_ref[ts:ts + CONV_PAD, :]
    for cb in range(3 * gw // LANES):
        cs = slice(cb * LANES, (cb + 1) * LANES)
        acc = None
        for i in range(GDN_CONV):
            off = CONV_PAD - (GDN_CONV - 1) + i
            term = cw_ref[i:i + 1, cs] * xpad_ref[off:off + ts, cs]
            acc = term if acc is None else acc + term
        cv_ref[:, cs] = acc * jax.nn.sigmoid(acc)

    ab = ab_ref[...]
    g = -jnp.exp(alog_ref[...]) * jax.nn.softplus(ab + dtb_ref[...])
    beta_all = jax.nn.sigmoid(ab)
    ri = lax.broadcasted_iota(jnp.int32, (ts, ts), 0)
    ci = lax.broadcasted_iota(jnp.int32, (ts, ts), 1)
    same_chunk = (ri // c) == (ci // c)
    causal = same_chunk & (ri >= ci)
    strict = same_chunk & (ri > ci)
    gcum = _dot_f32(causal.astype(F32), g)
    gcum_t = gcum.T
    g_last = jnp.concatenate(
        [jnp.broadcast_to(gcum[rc.stop - 1:rc.stop], (c, LANES)) for rc in chunks], axis=0)
    k_scale = jnp.exp(g_last - gcum)
    eg_all = jnp.exp(gcum)

    q, k, v, gc, beta = [], [], [], [], []
    for h in heads:
        qh = cv_ref[:, h * dh:(h + 1) * dh]
        kh = cv_ref[:, gw + h * dh:gw + (h + 1) * dh]
        q.append(qh * lax.rsqrt(jnp.sum(qh * qh, axis=-1, keepdims=True) + L2_EPS) * (dh ** -0.5))
        k.append(kh * lax.rsqrt(jnp.sum(kh * kh, axis=-1, keepdims=True) + L2_EPS))
        v.append(cv_ref[:, 2 * gw + h * dh:2 * gw + (h + 1) * dh])
        gc.append(gcum[:, h:h + 1])
        beta.append(beta_all[:, GDN_HEADS + h:GDN_HEADS + h + 1])
    qkk = [_dot_nt(jnp.concatenate([q[h], k[h]], axis=0).astype(BF16), k[h].astype(BF16)) for h in heads]
    decay = [jnp.exp(jnp.where(causal, gc[h] - gcum_t[h:h + 1, :], -jnp.inf)) for h in heads]
    qk = [qkk[h][:ts] * decay[h] for h in heads]
    m = [-jnp.where(strict, beta[h] * qkk[h][ts:] * decay[h], 0.0) for h in heads]
    sol = [jnp.concatenate([v[h] * beta[h], k[h] * (beta[h] * eg_all[:, h:h + 1])], axis=-1) for h in heads]
    n_rounds = c.bit_length() - 1
    for r in range(n_rounds):
        last = r + 1 == n_rounds
        mb = [m[h].astype(BF16) for h in heads]
        rhs = [sol[h].astype(BF16) if last else jnp.concatenate([sol[h], m[h]], axis=-1).astype(BF16)
               for h in heads]
        prod = [_dot(mb[h], rhs[h]) for h in heads]
        sol = [sol[h] + prod[h][:, :2 * dh] for h in heads]
        if not last:
            m = [prod[h][:, 2 * dh:] for h in heads]
    k_dec = [k[h] * k_scale[:, h:h + 1] for h in heads]
    q_dec = [q[h] * eg_all[:, h:h + 1] for h in heads]

    x = {}
    for ic, rc in enumerate(chunks):
        for h in heads:
            lhs = jnp.concatenate([k_dec[h][rc].T, qk[h][rc, rc]], axis=0).astype(BF16)
            x[ic, h] = _dot(lhs, sol[h][rc].astype(BF16))

    for ic, rc in enumerate(chunks):
        for h in heads:
            xs = x[ic, h]
            lhs = jnp.concatenate([-xs[:dh, dh:], q_dec[h][rc] - xs[dh:, dh:]], axis=0).astype(BF16)
            s = state_ref[h]
            y = _dot(lhs, s.astype(BF16))
            state_ref[h] = s * eg_all[rc.stop - 1:rc.stop, h:h + 1] + y[:dh] + xs[:dh, :dh]
            o = _rms(y[dh:] + xs[dh:, :dh], nw_ref[...])
            hs = slice(h * dh, (h + 1) * dh)
            z = z_ref[rc, hs].astype(F32)
            y_ref[rc, hs] = (o * (z * jax.nn.sigmoid(z))).astype(y_ref.dtype)


def _gdn(p, ab, conv_w, alog_row, dtb_row, norm_w, batch, seq):
    t = p.shape[0]
    ts = GDN_TS
    nj = seq // ts
    gw = GDN_W

    def rows(col_blk):
        return lambda b, j: (b * nj + j, col_blk)

    return pl.pallas_call(
        _gdn_kernel,
        out_shape=jax.ShapeDtypeStruct((t, gw), BF16),
        grid=(batch, nj),
        in_specs=[
            *[pl.BlockSpec((ts, GDN_QKV_BLOCK), rows(P_QB // GDN_QKV_BLOCK + i)) for i in range(GDN_QKV_BLOCKS)],
            pl.BlockSpec((ts, gw), rows(P_Z // gw)),
            pl.BlockSpec((ts, LANES), rows(0)),
            pl.BlockSpec((GDN_CONV, 3 * gw), lambda b, j: (0, 0)),
            pl.BlockSpec((1, LANES), lambda b, j: (0, 0)),
            pl.BlockSpec((1, LANES), lambda b, j: (0, 0)),
            pl.BlockSpec((1, GDN_HEAD_DIM), lambda b, j: (0, 0)),
        ],
        out_specs=pl.BlockSpec((ts, gw), rows(0)),
        scratch_shapes=[
            pltpu.VMEM((GDN_HEADS, GDN_HEAD_DIM, GDN_HEAD_DIM), F32),
            pltpu.VMEM((CONV_PAD, 3 * gw), F32),
            pltpu.VMEM((ts + CONV_PAD, 3 * gw), F32),
            pltpu.VMEM((ts, 3 * gw), F32),
        ],
        compiler_params=pltpu.CompilerParams(dimension_semantics=("arbitrary", "arbitrary")),
        name="gdn",
    )(*([p] * GDN_QKV_BLOCKS), p, ab, conv_w, alog_row, dtb_row, norm_w)


def _mix_kernel(sinks_ref, q_ref, kv_ref, kvp_ref, qc_ref, mk_ref, mv_ref, x_ref, ga_ref, gb_ref, gc_ref, yb_ref,
                wa_ref, wb_ref, wc_ref, wo_ref, w1_ref, w2_ref, h_ref, w1b_ref, w2b_ref, y0_ref, y1_ref, *,
                tiles_per_seq, n_tiles):
    i = pl.program_id(0)
    w = SWA_WINDOW
    hd = SWA_HEAD_DIM

    @pl.when(i == 0)
    def _():
        y0_ref[...] = jnp.zeros_like(y0_ref)

    def step(yprev_ref, ynext_ref):
        w1b_ref[...] = w1_ref[...].astype(BF16)
        w2b_ref[...] = w2_ref[...].astype(BF16)

        first = (jnp.minimum(i, n_tiles - 1) % tiles_per_seq) == 0
        kvw = SWA_KV_W
        k_all = (jnp.concatenate([kvp_ref[:, :kvw], kv_ref[:, :kvw]], axis=0).astype(F32)
                 * (hd ** -0.5)).astype(BF16)
        v_all = jnp.concatenate([kvp_ref[:, kvw:], kv_ref[:, kvw:]], axis=0)
        for b in range(q_ref.shape[0] // w):
            _swa_block(sinks_ref, q_ref, slice(b * w, (b + 1) * w), k_all[b * w:(b + 2) * w],
                       v_all[b * w:(b + 2) * w], first if b == 0 else False, ynext_ref)
        _xattn_rows(qc_ref, mk_ref, mv_ref, ynext_ref, SWA_Q_W)

        merged = jax.nn.sigmoid(ga_ref[...].astype(F32)) * _dot(yprev_ref[:, :SWA_Q_W], wa_ref[...])
        merged += jax.nn.sigmoid(gb_ref[...].astype(F32)) * _dot(yb_ref[...], wb_ref[...])
        merged += jax.nn.sigmoid(gc_ref[...].astype(F32)) * _dot(yprev_ref[:, SWA_Q_W:], wc_ref[...])
        h_ref[...] = x_ref[...] + _dot(merged.astype(BF16), wo_ref[...])

    @pl.when(i % 2 == 0)
    def _():
        step(y0_ref, y1_ref)

    @pl.when(i % 2 == 1)
    def _():
        step(y1_ref, y0_ref)


def _mix(x, p, mkv, yb, sinks, wa, wb, wc, wo, w1, w2, seq, *, tm=256):
    t, d = x.shape
    w = SWA_WINDOW
    n_tiles = t // tm
    tiles_per_seq = seq // tm
    kv_w = 2 * SWA_KV_W
    assert P_VA == P_KA + SWA_KV_W and P_KA % kv_w == 0

    def att(i):
        return jnp.minimum(i, n_tiles - 1)

    def mm(i):
        return jnp.maximum(i - 1, 0)

    def resident(shape):
        return pl.BlockSpec(shape, lambda i: (0, 0), pipeline_mode=pl.Buffered(1))

    def row_slab(wf):
        r, c = wf.shape
        assert r % (n_tiles * BF16_SUBLANES) == 0
        return pl.BlockSpec((r // n_tiles, c), lambda i: (att(i), 0))

    return pl.pallas_call(
        functools.partial(_mix_kernel, tiles_per_seq=tiles_per_seq, n_tiles=n_tiles),
        out_shape=(jax.ShapeDtypeStruct((t, d), F32), jax.ShapeDtypeStruct(w1.shape, BF16),
                   jax.ShapeDtypeStruct(w2.shape, BF16)),
        grid=(n_tiles + 1,),
        in_specs=[
            pl.BlockSpec(memory_space=pltpu.SMEM),
            pl.BlockSpec((tm, SWA_Q_W), lambda i: (att(i), P_QA // SWA_Q_W)),
            pl.BlockSpec((tm, kv_w), lambda i: (att(i), P_KA // kv_w)),
            pl.BlockSpec((w, kv_w), lambda i: (jnp.maximum(att(i) * (tm // w) - 1, 0), P_KA // kv_w)),
            pl.BlockSpec((tm, XA_W), lambda i: (att(i), P_QC // XA_W)),
            pl.BlockSpec((N_MEM, XA_W), lambda i: (att(i) // tiles_per_seq, 0)),
            pl.BlockSpec((N_MEM, XA_W), lambda i: (att(i) // tiles_per_seq, 1)),
            pl.BlockSpec((tm, d), lambda i: (mm(i), 0)),
            pl.BlockSpec((tm, d), lambda i: (mm(i), P_GATE // d)),
            pl.BlockSpec((tm, d), lambda i: (mm(i), P_GATE // d + 1)),
            pl.BlockSpec((tm, d), lambda i: (mm(i), P_GATE // d + 2)),
            pl.BlockSpec((tm, GDN_W), lambda i: (mm(i), 0)),
            resident((SWA_Q_W, d)),
            resident((GDN_W, d)),
            resident((XA_W, d)),
            resident((d, d)),
            row_slab(w1),
            row_slab(w2),
        ],
        out_specs=(pl.BlockSpec((tm, d), lambda i: (mm(i), 0)), row_slab(w1), row_slab(w2)),
        scratch_shapes=[pltpu.VMEM((tm, SWA_Q_W + XA_W), BF16), pltpu.VMEM((tm, SWA_Q_W + XA_W), BF16)],
        compiler_params=pltpu.CompilerParams(
            dimension_semantics=("arbitrary",), vmem_limit_bytes=VMEM_LIMIT),
        name="mix",
    )(sinks, p, p, p, p, mkv, mkv, x, p, p, p, yb, wa, wb, wc, wo, w1, w2)


def _mlp_kernel(h_ref, g_ref, w1_ref, w2_ref, gf_ref, o_ref, n_ref, acc_ref, *, final_norm):
    j = pl.program_id(1)

    @pl.when(j == 0)
    def _():
        h = h_ref[...]
        n_ref[...] = _rms(h, g_ref[...]).astype(BF16)
        acc_ref[...] = h

    u = _dot(n_ref[...], w1_ref[...])
    acc_ref[...] += _dot(jnp.square(jnp.maximum(u, 0.0)).astype(BF16), w2_ref[...])

    @pl.when(j == pl.num_programs(1) - 1)
    def _():
        h = acc_ref[...]
        o_ref[...] = _rms(h, gf_ref[...]) if final_norm else h


def _mlp(h, g, w1, w2, g_final, *, final_norm, tm=512, tf=1024):
    t, d = h.shape
    f = w1.shape[1]
    return pl.pallas_call(
        functools.partial(_mlp_kernel, final_norm=final_norm),
        out_shape=jax.ShapeDtypeStruct((t, d), F32),
        grid=(t // tm, f // tf),
        in_specs=[
            pl.BlockSpec((tm, d), lambda i, j: (i, 0)),
            pl.BlockSpec((1, d), lambda i, j: (0, 0)),
            pl.BlockSpec((d, tf), lambda i, j: (0, j)),
            pl.BlockSpec((tf, d), lambda i, j: (j, 0)),
            pl.BlockSpec((1, d), lambda i, j: (0, 0)),
        ],
        out_specs=pl.BlockSpec((tm, d), lambda i, j: (i, 0)),
        scratch_shapes=[pltpu.VMEM((tm, d), BF16), pltpu.VMEM((tm, d), F32)],
        compiler_params=pltpu.CompilerParams(
            dimension_semantics=("arbitrary", "arbitrary"), vmem_limit_bytes=VMEM_LIMIT),
        name="mlp",
    )(h, g, w1, w2, g_final)


def _lane_row(v):
    return jnp.zeros((1, LANES), F32).at[0, :v.shape[0]].set(v.astype(F32))


def kernel(x, mem, g_mix, w_in, sinks, conv_w, a_log, dt_bias, gdn_norm_w, g_mem, w_mem_kv, w_swa_up,
           w_gdn_up, w_xa_up, w_out, g_mlp, w_mlp_in, w_mlp_out, g_final):
    batch, seq, d = x.shape
    depth = w_in.shape[0]
    h = x.reshape(batch * seq, d)
    mem2 = mem.reshape(batch * N_MEM, d)
    for l in range(depth):
        p, ab, wo, wa, wb, wc, wm = _in_proj(
            h, g_mix[l][None], jnp.swapaxes(w_in[l], 0, 1).astype(BF16), w_out[l], w_swa_up[l], w_gdn_up[l],
            w_xa_up[l], w_mem_kv[l])
        mkv = _mem_kv(mem2, g_mem[l][None], wm)
        yb = _gdn(p, ab, conv_w[l], _lane_row(a_log[l]), _lane_row(dt_bias[l]), gdn_norm_w[l][None],
                  batch, seq)
        h, w1, w2 = _mix(h, p, mkv, yb, sinks[l], wa, wb, wc, wo, w_mlp_in[l], w_mlp_out[l], seq)
        h = _mlp(h, g_mlp[l][None], w1, w2, g_final[None], final_norm=(l == depth - 1))
    return h.reshape(batch, seq, d)
```

```python
import functools

import jax
import jax.numpy as jnp
from jax import lax
from jax.experimental import pallas as pl
from jax.experimental.pallas import tpu as pltpu

F32 = jnp.float32
BF16 = jnp.bfloat16

D_MODEL = 2048
SWA_Q_HEADS = 16
SWA_KV_HEADS = 2
SWA_HEAD_DIM = 64
SWA_WINDOW = 128
GDN_HEADS = 4
GDN_HEAD_DIM = 128
GDN_CONV = 4
GDN_CHUNK = 64
N_MEM = 256
XA_HEADS = 4
XA_HEAD_DIM = 128
D_FF = 4 * D_MODEL
RMS_EPS = 1e-6
L2_EPS = 1e-6

SWA_Q_W = SWA_Q_HEADS * SWA_HEAD_DIM
SWA_KV_W = SWA_KV_HEADS * SWA_HEAD_DIM
GDN_W = GDN_HEADS * GDN_HEAD_DIM
XA_W = XA_HEADS * XA_HEAD_DIM

LANES = 128
SUBLANES = 8
BF16_SUBLANES = 16

AB_W = 2 * GDN_HEADS

R_QA = 0
R_KA = R_QA + SWA_Q_W
R_VA = R_KA + SWA_KV_W
R_QB = R_VA + SWA_KV_W
R_AB = R_QB + 3 * GDN_W
R_Z = R_AB + AB_W
R_QC = R_Z + GDN_W
R_GATE = R_QC + XA_W
R_END = R_GATE + 3 * D_MODEL

IN_PROJ_TN = 1024
IN_PROJ_TILE_STARTS = (
    tuple(R_GATE + k * IN_PROJ_TN for k in range(3 * D_MODEL // IN_PROJ_TN))
    + tuple(range(0, R_Z, IN_PROJ_TN))
    + tuple(R_Z + k * IN_PROJ_TN for k in range((R_GATE - R_Z) // IN_PROJ_TN)))
P_WIDTH = len(IN_PROJ_TILE_STARTS) * IN_PROJ_TN
P_GATE = 0
P_FRONT = 3 * D_MODEL
P_QA = P_FRONT + R_QA
P_KA = P_FRONT + R_KA
P_VA = P_FRONT + R_VA
P_QB = P_FRONT + R_QB
P_AB = P_FRONT + R_AB
P_Z = P_FRONT + -(-R_Z // IN_PROJ_TN) * IN_PROJ_TN
P_QC = P_Z + GDN_W
assert (R_GATE - R_Z) % IN_PROJ_TN == 0 and (3 * D_MODEL) % IN_PROJ_TN == 0
assert all(s % SUBLANES == 0 for s in IN_PROJ_TILE_STARTS) and P_AB % LANES == 0 and P_QC + XA_W == P_WIDTH

VMEM_LIMIT = 56 * 1024 * 1024


def _rms(x, g):
    return x * lax.rsqrt(jnp.mean(x * x, axis=-1, keepdims=True) + RMS_EPS) * g


def _dot(a, b):
    return jnp.dot(a, b, preferred_element_type=F32)


def _dot_nt(a, b):
    return lax.dot_general(a, b, (((1,), (1,)), ((), ())), preferred_element_type=F32)


def _dot_f32(a, b):
    return jnp.dot(a, b, preferred_element_type=F32, precision=lax.Precision.HIGHEST)


def _in_proj_kernel(starts_ref, x_ref, g_ref, wt_ref, o_ref, ab_ref, n_ref, *, ab_tile, ab_off):
    del starts_ref
    j = pl.program_id(1)

    @pl.when(j == 0)
    def _():
        n_ref[...] = _rms(x_ref[...], g_ref[...]).astype(BF16)

    acc = _dot_nt(n_ref[...], wt_ref[...].astype(BF16))
    o_ref[...] = acc.astype(o_ref.dtype)

    @pl.when(j == ab_tile)
    def _():
        ab_ref[...] = acc[:, ab_off:ab_off + LANES]


def _in_proj(x, g, wt, *, tm=1024, tn=IN_PROJ_TN):
    t, d = x.shape
    return pl.pallas_call(
        functools.partial(_in_proj_kernel, ab_tile=P_AB // tn, ab_off=P_AB % tn),
        out_shape=(jax.ShapeDtypeStruct((t, P_WIDTH), BF16), jax.ShapeDtypeStruct((t, LANES), F32)),
        grid_spec=pltpu.PrefetchScalarGridSpec(
            num_scalar_prefetch=1,
            grid=(t // tm, P_WIDTH // tn),
            in_specs=[
                pl.BlockSpec((tm, d), lambda i, j, starts: (i, 0)),
                pl.BlockSpec((1, d), lambda i, j, starts: (0, 0)),
                pl.BlockSpec((pl.Element(tn), pl.Element(d)),
                             lambda i, j, starts: (pl.multiple_of(starts[j], SUBLANES), 0)),
            ],
            out_specs=(pl.BlockSpec((tm, tn), lambda i, j, starts: (i, j)),
                       pl.BlockSpec((tm, LANES), lambda i, j, starts: (i, 0))),
            scratch_shapes=[pltpu.VMEM((tm, d), BF16)],
        ),
        compiler_params=pltpu.CompilerParams(
            dimension_semantics=("arbitrary", "arbitrary"), vmem_limit_bytes=VMEM_LIMIT),
        name="in_proj",
    )(jnp.asarray(IN_PROJ_TILE_STARTS, jnp.int32), x, g, wt)


def _mem_kv_kernel(m_ref, g_ref, w_ref, o_ref):
    o_ref[...] = _dot(_rms(m_ref[...], g_ref[...]).astype(BF16), w_ref[...].astype(BF16)).astype(o_ref.dtype)


def _mem_kv(mem, g, w):
    t, d = mem.shape
    n = w.shape[1]
    return pl.pallas_call(
        _mem_kv_kernel,
        out_shape=jax.ShapeDtypeStruct((t, n), BF16),
        grid=(t // N_MEM,),
        in_specs=[
            pl.BlockSpec((N_MEM, d), lambda i: (i, 0)),
            pl.BlockSpec((1, d), lambda i: (0, 0)),
            pl.BlockSpec((d, n), lambda i: (0, 0)),
        ],
        out_specs=pl.BlockSpec((N_MEM, n), lambda i: (i, 0)),
        compiler_params=pltpu.CompilerParams(
            dimension_semantics=("arbitrary",), vmem_limit_bytes=VMEM_LIMIT),
        name="mem_kv",
    )(mem, g, w)


def _swa_block(sinks_ref, q_ref, rows, k, v, first, o_ref):
    w = SWA_WINDOW
    hd = SWA_HEAD_DIM
    group = SWA_Q_HEADS // SWA_KV_HEADS
    ones = jnp.ones((2 * w, hd), BF16)
    qi = lax.broadcasted_iota(jnp.int32, (w, 2 * w), 0)
    kj = lax.broadcasted_iota(jnp.int32, (w, 2 * w), 1)
    valid = (kj > qi) & (kj <= qi + w)
    if first is not False:
        valid = valid & (kj >= jnp.where(first, w, 0))
    heads = range(SWA_Q_HEADS)
    kh = [k[:, hk * hd:(hk + 1) * hd] for hk in range(SWA_KV_HEADS)]
    vh = [jnp.concatenate([v[:, hk * hd:(hk + 1) * hd], ones], axis=1) for hk in range(SWA_KV_HEADS)]
    s = [_dot_nt(q_ref[rows, h * hd:(h + 1) * hd], kh[h // group]) for h in heads]
    s = [jnp.where(valid, s[h], -jnp.inf) for h in heads]
    m = [jnp.maximum(jnp.max(s[h], axis=-1, keepdims=True), sinks_ref[h]) for h in heads]
    p = [jnp.exp(s[h] - m[h]).astype(BF16) for h in heads]
    ov = [_dot(p[h], vh[h // group]) for h in heads]
    for h in heads:
        o = ov[h][:, :hd] / (ov[h][:, hd:hd + 1] + jnp.exp(sinks_ref[h] - m[h]))
        o_ref[rows, h * hd:(h + 1) * hd] = o.astype(o_ref.dtype)


XA_SUB = 128


def _xattn_rows(q_ref, mk_ref, mv_ref, o_ref, col0):
    hd = XA_HEAD_DIM
    ones = jnp.ones((N_MEM, hd), BF16)
    units = [(slice(r, r + XA_SUB), h) for r in range(0, q_ref.shape[0], XA_SUB) for h in range(XA_HEADS)]
    mv = [jnp.concatenate([mv_ref[:, h * hd:(h + 1) * hd], ones], axis=1) for h in range(XA_HEADS)]
    s = [_dot_nt(q_ref[rs, h * hd:(h + 1) * hd], mk_ref[:, h * hd:(h + 1) * hd]) * (hd ** -0.5)
         for rs, h in units]
    e = [jnp.exp(si - jnp.max(si, axis=-1, keepdims=True)).astype(BF16) for si in s]
    oe = [_dot(ei, mv[h]) for ei, (_, h) in zip(e, units)]
    for (rs, h), oi in zip(units, oe):
        o_ref[rs, col0 + h * hd:col0 + (h + 1) * hd] = (oi[:, :hd] / oi[:, hd:hd + 1]).astype(o_ref.dtype)


GDN_TS = 256
GDN_QKV_BLOCK = 256
GDN_QKV_BLOCKS = 3 * GDN_W // GDN_QKV_BLOCK
assert P_QB % GDN_QKV_BLOCK == 0
CONV_PAD = SUBLANES


GDN_CAST_WEIGHTS = 4


def _gdn_kernel(*refs):
    qkv_refs = refs[:GDN_QKV_BLOCKS]
    refs = refs[GDN_QKV_BLOCKS:]
    z_ref, ab_ref, cw_ref, alog_ref, dtb_ref, nw_ref = refs[:6]
    wf_refs = refs[6:6 + GDN_CAST_WEIGHTS]
    y_ref = refs[6 + GDN_CAST_WEIGHTS]
    wb_refs = refs[7 + GDN_CAST_WEIGHTS:7 + 2 * GDN_CAST_WEIGHTS]
    state_ref, carry_ref, xpad_ref, cv_ref = refs[7 + 2 * GDN_CAST_WEIGHTS:]
    for wf_ref, wb_ref in zip(wf_refs, wb_refs):
        wb_ref[...] = wf_ref[...].astype(BF16)
    ts = GDN_TS
    c = GDN_CHUNK
    dh = GDN_HEAD_DIM
    gw = GDN_W
    heads = range(GDN_HEADS)
    chunks = [slice(i * c, (i + 1) * c) for i in range(ts // c)]

    @pl.when(pl.program_id(1) == 0)
    def _():
        state_ref[...] = jnp.zeros_like(state_ref)
        carry_ref[...] = jnp.zeros_like(carry_ref)

    xpad_ref[0:CONV_PAD, :] = carry_ref[...]
    for i, r in enumerate(qkv_refs):
        xpad_ref[CONV_PAD:, i * GDN_QKV_BLOCK:(i + 1) * GDN_QKV_BLOCK] = r[...].astype(F32)
    carry_ref[...] = xpad_ref[ts:ts + CONV_PAD, :]
    for cb in range(3 * gw // LANES):
        cs = slice(cb * LANES, (cb + 1) * LANES)
        acc = None
        for i in range(GDN_CONV):
            off = CONV_PAD - (GDN_CONV - 1) + i
            term = cw_ref[i:i + 1, cs] * xpad_ref[off:off + ts, cs]
            acc = term if acc is None else acc + term
        cv_ref[:, cs] = acc * jax.nn.sigmoid(acc)

    ab = ab_ref[...]
    g = -jnp.exp(alog_ref[...]) * jax.nn.softplus(ab + dtb_ref[...])
    beta_all = jax.nn.sigmoid(ab)
    ri = lax.broadcasted_iota(jnp.int32, (ts, ts), 0)
    ci = lax.broadcasted_iota(jnp.int32, (ts, ts), 1)
    same_chunk = (ri // c) == (ci // c)
    causal = same_chunk & (ri >= ci)
    strict = same_chunk & (ri > ci)
    gcum = _dot_f32(causal.astype(F32), g)
    gcum_t = gcum.T
    g_last = jnp.concatenate(
        [jnp.broadcast_to(gcum[rc.stop - 1:rc.stop], (c, LANES)) for rc in chunks], axis=0)
    k_scale = jnp.exp(g_last - gcum)
    eg_all = jnp.exp(gcum)

    q, k, v, gc, beta = [], [], [], [], []
    for h in heads:
        qh = cv_ref[:, h * dh:(h + 1) * dh]
        kh = cv_ref[:, gw + h * dh:gw + (h + 1) * dh]
        q.append(qh * lax.rsqrt(jnp.sum(qh * qh, axis=-1, keepdims=True) + L2_EPS) * (dh ** -0.5))
        k.append(kh * lax.rsqrt(jnp.sum(kh * kh, axis=-1, keepdims=True) + L2_EPS))
        v.append(cv_ref[:, 2 * gw + h * dh:2 * gw + (h + 1) * dh])
        gc.append(gcum[:, h:h + 1])
        beta.append(beta_all[:, GDN_HEADS + h:GDN_HEADS + h + 1])
    qkk = [_dot_nt(jnp.concatenate([q[h], k[h]], axis=0).astype(BF16), k[h].astype(BF16)) for h in heads]
    decay = [jnp.exp(jnp.where(causal, gc[h] - gcum_t[h:h + 1, :], -jnp.inf)) for h in heads]
    qk = [qkk[h][:ts] * decay[h] for h in heads]
    m = [-jnp.where(strict, beta[h] * qkk[h][ts:] * decay[h], 0.0) for h in heads]
    sol = [jnp.concatenate([v[h] * beta[h], k[h] * (beta[h] * eg_all[:, h:h + 1])], axis=-1) for h in heads]
    n_rounds = c.bit_length() - 1
    for r in range(n_rounds):
        last = r + 1 == n_rounds
        mb = [m[h].astype(BF16) for h in heads]
        rhs = [sol[h].astype(BF16) if last else jnp.concatenate([sol[h], m[h]], axis=-1).astype(BF16)
               for h in heads]
        prod = [_dot(mb[h], rhs[h]) for h in heads]
        sol = [sol[h] + prod[h][:, :2 * dh] for h in heads]
        if not last:
            m = [prod[h][:, 2 * dh:] for h in heads]
    k_dec = [k[h] * k_scale[:, h:h + 1] for h in heads]
    q_dec = [q[h] * eg_all[:, h:h + 1] for h in heads]

    x = {}
    for ic, rc in enumerate(chunks):
        for h in heads:
            lhs = jnp.concatenate([k_dec[h][rc].T, qk[h][rc, rc]], axis=0).astype(BF16)
            x[ic, h] = _dot(lhs, sol[h][rc].astype(BF16))

    for ic, rc in enumerate(chunks):
        for h in heads:
            xs = x[ic, h]
            lhs = jnp.concatenate([-xs[:dh, dh:], q_dec[h][rc] - xs[dh:, dh:]], axis=0).astype(BF16)
            s = state_ref[h]
            y = _dot(lhs, s.astype(BF16))
            state_ref[h] = s * eg_all[rc.stop - 1:rc.stop, h:h + 1] + y[:dh] + xs[:dh, :dh]
            o = _rms(y[dh:] + xs[dh:, :dh], nw_ref[...])
            hs = slice(h * dh, (h + 1) * dh)
            z = z_ref[rc, hs].astype(F32)
            y_ref[rc, hs] = (o * (z * jax.nn.sigmoid(z))).astype(y_ref.dtype)


def _gdn(p, ab, conv_w, alog_row, dtb_row, norm_w, cast_weights, batch, seq):
    t = p.shape[0]
    ts = GDN_TS
    nj = seq // ts
    gw = GDN_W
    n_steps = batch * nj
    assert len(cast_weights) == GDN_CAST_WEIGHTS

    def rows(col_blk):
        return lambda b, j: (b * nj + j, col_blk)

    def row_slab(wf):
        r, c = wf.shape
        slab = max(r // n_steps, BF16_SUBLANES)
        assert r % slab == 0 and n_steps % (r // slab) == 0
        reuse = n_steps // (r // slab)
        return pl.BlockSpec((slab, c), lambda b, j: ((b * nj + j) // reuse, 0))

    w_specs = [row_slab(wf) for wf in cast_weights]
    return pl.pallas_call(
        _gdn_kernel,
        out_shape=(jax.ShapeDtypeStruct((t, gw), BF16), *[jax.ShapeDtypeStruct(wf.shape, BF16) for wf in cast_weights]),
        grid=(batch, nj),
        in_specs=[
            *[pl.BlockSpec((ts, GDN_QKV_BLOCK), rows(P_QB // GDN_QKV_BLOCK + i)) for i in range(GDN_QKV_BLOCKS)],
            pl.BlockSpec((ts, gw), rows(P_Z // gw)),
            pl.BlockSpec((ts, LANES), rows(0)),
            pl.BlockSpec((GDN_CONV, 3 * gw), lambda b, j: (0, 0)),
            pl.BlockSpec((1, LANES), lambda b, j: (0, 0)),
            pl.BlockSpec((1, LANES), lambda b, j: (0, 0)),
            pl.BlockSpec((1, GDN_HEAD_DIM), lambda b, j: (0, 0)),
            *w_specs,
        ],
        out_specs=(pl.BlockSpec((ts, gw), rows(0)), *w_specs),
        scratch_shapes=[
            pltpu.VMEM((GDN_HEADS, GDN_HEAD_DIM, GDN_HEAD_DIM), F32),
            pltpu.VMEM((CONV_PAD, 3 * gw), F32),
            pltpu.VMEM((ts + CONV_PAD, 3 * gw), F32),
            pltpu.VMEM((ts, 3 * gw), F32),
        ],
        compiler_params=pltpu.CompilerParams(dimension_semantics=("arbitrary", "arbitrary")),
        name="gdn",
    )(*([p] * GDN_QKV_BLOCKS), p, ab, conv_w, alog_row, dtb_row, norm_w, *cast_weights)


def _mix_kernel(sinks_ref, q_ref, kv_ref, kvp_ref, qc_ref, mk_ref, mv_ref, x_ref, ga_ref, gb_ref, gc_ref, yb_ref,
                wa_ref, wb_ref, wc_ref, wo_ref, w1_ref, w2_ref, h_ref, w1b_ref, w2b_ref, y0_ref, y1_ref, *,
                tiles_per_seq, n_tiles):
    i = pl.program_id(0)
    w = SWA_WINDOW
    hd = SWA_HEAD_DIM

    @pl.when(i == 0)
    def _():
        y0_ref[...] = jnp.zeros_like(y0_ref)

    def step(yprev_ref, ynext_ref):
        w1b_ref[...] = w1_ref[...].astype(BF16)
        w2b_ref[...] = w2_ref[...].astype(BF16)

        first = (jnp.minimum(i, n_tiles - 1) % tiles_per_seq) == 0
        kvw = SWA_KV_W
        k_all = (jnp.concatenate([kvp_ref[:, :kvw], kv_ref[:, :kvw]], axis=0).astype(F32)
                 * (hd ** -0.5)).astype(BF16)
        v_all = jnp.concatenate([kvp_ref[:, kvw:], kv_ref[:, kvw:]], axis=0)
        for b in range(q_ref.shape[0] // w):
            _swa_block(sinks_ref, q_ref, slice(b * w, (b + 1) * w), k_all[b * w:(b + 2) * w],
                       v_all[b * w:(b + 2) * w], first if b == 0 else False, ynext_ref)
        _xattn_rows(qc_ref, mk_ref, mv_ref, ynext_ref, SWA_Q_W)

        merged = jax.nn.sigmoid(ga_ref[...].astype(F32)) * _dot(yprev_ref[:, :SWA_Q_W], wa_ref[...])
        merged += jax.nn.sigmoid(gb_ref[...].astype(F32)) * _dot(yb_ref[...], wb_ref[...])
        merged += jax.nn.sigmoid(gc_ref[...].astype(F32)) * _dot(yprev_ref[:, SWA_Q_W:], wc_ref[...])
        h_ref[...] = x_ref[...] + _dot(merged.astype(BF16), wo_ref[...])

    @pl.when(i % 2 == 0)
    def _():
        step(y0_ref, y1_ref)

    @pl.when(i % 2 == 1)
    def _():
        step(y1_ref, y0_ref)


def _mix(x, p, mkv, yb, sinks, wa, wb, wc, wo, w1, w2, seq, *, tm=256):
    t, d = x.shape
    w = SWA_WINDOW
    n_tiles = t // tm
    tiles_per_seq = seq // tm
    kv_w = 2 * SWA_KV_W
    assert P_VA == P_KA + SWA_KV_W and P_KA % kv_w == 0

    def att(i):
        return jnp.minimum(i, n_tiles - 1)

    def mm(i):
        return jnp.maximum(i - 1, 0)

    def resident(shape):
        return pl.BlockSpec(shape, lambda i: (0, 0), pipeline_mode=pl.Buffered(1))

    def row_slab(wf):
        r, c = wf.shape
        assert r % (n_tiles * BF16_SUBLANES) == 0
        return pl.BlockSpec((r // n_tiles, c), lambda i: (att(i), 0))

    return pl.pallas_call(
        functools.partial(_mix_kernel, tiles_per_seq=tiles_per_seq, n_tiles=n_tiles),
        out_shape=(jax.ShapeDtypeStruct((t, d), F32), jax.ShapeDtypeStruct(w1.shape, BF16),
                   jax.ShapeDtypeStruct(w2.shape, BF16)),
        grid=(n_tiles + 1,),
        in_specs=[
            pl.BlockSpec(memory_space=pltpu.SMEM),
            pl.BlockSpec((tm, SWA_Q_W), lambda i: (att(i), P_QA // SWA_Q_W)),
            pl.BlockSpec((tm, kv_w), lambda i: (att(i), P_KA // kv_w)),
            pl.BlockSpec((w, kv_w), lambda i: (jnp.maximum(att(i) * (tm // w) - 1, 0), P_KA // kv_w)),
            pl.BlockSpec((tm, XA_W), lambda i: (att(i), P_QC // XA_W)),
            pl.BlockSpec((N_MEM, XA_W), lambda i: (att(i) // tiles_per_seq, 0)),
            pl.BlockSpec((N_MEM, XA_W), lambda i: (att(i) // tiles_per_seq, 1)),
            pl.BlockSpec((tm, d), lambda i: (mm(i), 0)),
            pl.BlockSpec((tm, d), lambda i: (mm(i), P_GATE // d)),
            pl.BlockSpec((tm, d), lambda i: (mm(i), P_GATE // d + 1)),
            pl.BlockSpec((tm, d), lambda i: (mm(i), P_GATE // d + 2)),
            pl.BlockSpec((tm, GDN_W), lambda i: (mm(i), 0)),
            resident((SWA_Q_W, d)),
            resident((GDN_W, d)),
            resident((XA_W, d)),
            resident((d, d)),
            row_slab(w1),
            row_slab(w2),
        ],
        out_specs=(pl.BlockSpec((tm, d), lambda i: (mm(i), 0)), row_slab(w1), row_slab(w2)),
        scratch_shapes=[pltpu.VMEM((tm, SWA_Q_W + XA_W), BF16), pltpu.VMEM((tm, SWA_Q_W + XA_W), BF16)],
        compiler_params=pltpu.CompilerParams(
            dimension_semantics=("arbitrary",), vmem_limit_bytes=VMEM_LIMIT),
        name="mix",
    )(sinks, p, p, p, p, mkv, mkv, x, p, p, p, yb, wa, wb, wc, wo, w1, w2)


def _mlp_kernel(h_ref, g_ref, w1_ref, w2_ref, gf_ref, o_ref, n_ref, acc_ref, *, final_norm):
    j = pl.program_id(1)

    @pl.when(j == 0)
    def _():
        h = h_ref[...]
        n_ref[...] = _rms(h, g_ref[...]).astype(BF16)
        acc_ref[...] = h

    u = _dot(n_ref[...], w1_ref[...])
    acc_ref[...] += _dot(jnp.square(jnp.maximum(u, 0.0)).astype(BF16), w2_ref[...])

    @pl.when(j == pl.num_programs(1) - 1)
    def _():
        h = acc_ref[...]
        o_ref[...] = _rms(h, gf_ref[...]) if final_norm else h


def _mlp(h, g, w1, w2, g_final, *, final_norm, tm=512, tf=1024):
    t, d = h.shape
    f = w1.shape[1]
    return pl.pallas_call(
        functools.partial(_mlp_kernel, final_norm=final_norm),
        out_shape=jax.ShapeDtypeStruct((t, d), F32),
        grid=(t // tm, f // tf),
        in_specs=[
            pl.BlockSpec((tm, d), lambda i, j: (i, 0)),
            pl.BlockSpec((1, d), lambda i, j: (0, 0)),
            pl.BlockSpec((d, tf), lambda i, j: (0, j)),
            pl.BlockSpec((tf, d), lambda i, j: (j, 0)),
            pl.BlockSpec((1, d), lambda i, j: (0, 0)),
        ],
        out_specs=pl.BlockSpec((tm, d), lambda i, j: (i, 0)),
        scratch_shapes=[pltpu.VMEM((tm, d), BF16), pltpu.VMEM((tm, d), F32)],
        compiler_params=pltpu.CompilerParams(
            dimension_semantics=("arbitrary", "arbitrary"), vmem_limit_bytes=VMEM_LIMIT),
        name="mlp",
    )(h, g, w1, w2, g_final)


def _lane_row(v):
    return jnp.zeros((1, LANES), F32).at[0, :v.shape[0]].set(v.astype(F32))


def kernel(x, mem, g_mix, w_in, sinks, conv_w, a_log, dt_bias, gdn_norm_w, g_mem, w_mem_kv, w_swa_up,
           w_gdn_up, w_xa_up, w_out, g_mlp, w_mlp_in, w_mlp_out, g_final):
    batch, seq, d = x.shape
    depth = w_in.shape[0]
    h = x.reshape(batch * seq, d)
    mem2 = mem.reshape(batch * N_MEM, d)
    for l in range(depth):
        p, ab = _in_proj(h, g_mix[l][None], jnp.swapaxes(w_in[l], 0, 1))
        mkv = _mem_kv(mem2, g_mem[l][None], w_mem_kv[l])
        yb, wo, wa, wb, wc = _gdn(p, ab, conv_w[l], _lane_row(a_log[l]), _lane_row(dt_bias[l]),
                                  gdn_norm_w[l][None], (w_out[l], w_swa_up[l], w_gdn_up[l], w_xa_up[l]),
                                  batch, seq)
        h, w1, w2 = _mix(h, p, mkv, yb, sinks[l], wa, wb, wc, wo, w_mlp_in[l], w_mlp_out[l], seq)
        h = _mlp(h, g_mlp[l][None], w1, w2, g_final[None], final_norm=(l == depth - 1))
    return h.reshape(batch, seq, d)
```

```python
import functools

import jax
import jax.numpy as jnp
from jax import lax
from jax.experimental import pallas as pl
from jax.experimental.pallas import tpu as pltpu

F32 = jnp.float32
BF16 = jnp.bfloat16

D_MODEL = 2048
SWA_Q_HEADS = 16
SWA_KV_HEADS = 2
SWA_HEAD_DIM = 64
SWA_WINDOW = 128
GDN_HEADS = 4
GDN_HEAD_DIM = 128
GDN_CONV = 4
GDN_CHUNK = 64
N_MEM = 256
XA_HEADS = 4
XA_HEAD_DIM = 128
D_FF = 4 * D_MODEL
RMS_EPS = 1e-6
L2_EPS = 1e-6

SWA_Q_W = SWA_Q_HEADS * SWA_HEAD_DIM
SWA_KV_W = SWA_KV_HEADS * SWA_HEAD_DIM
GDN_W = GDN_HEADS * GDN_HEAD_DIM
XA_W = XA_HEADS * XA_HEAD_DIM

LANES = 128
SUBLANES = 8
BF16_SUBLANES = 16

AB_W = 2 * GDN_HEADS

R_QA = 0
R_KA = R_QA + SWA_Q_W
R_VA = R_KA + SWA_KV_W
R_QB = R_VA + SWA_KV_W
R_AB = R_QB + 3 * GDN_W
R_Z = R_AB + AB_W
R_QC = R_Z + GDN_W
R_GATE = R_QC + XA_W
R_END = R_GATE + 3 * D_MODEL

IN_PROJ_TN = 1024
IN_PROJ_TILE_STARTS = (
    tuple(R_GATE + k * IN_PROJ_TN for k in range(3 * D_MODEL // IN_PROJ_TN))
    + tuple(range(0, R_Z, IN_PROJ_TN))
    + tuple(R_Z + k * IN_PROJ_TN for k in range((R_GATE - R_Z) // IN_PROJ_TN)))
P_WIDTH = len(IN_PROJ_TILE_STARTS) * IN_PROJ_TN
P_GATE = 0
P_FRONT = 3 * D_MODEL
P_QA = P_FRONT + R_QA
P_KA = P_FRONT + R_KA
P_VA = P_FRONT + R_VA
P_QB = P_FRONT + R_QB
P_AB = P_FRONT + R_AB
P_Z = P_FRONT + -(-R_Z // IN_PROJ_TN) * IN_PROJ_TN
P_QC = P_Z + GDN_W
assert (R_GATE - R_Z) % IN_PROJ_TN == 0 and (3 * D_MODEL) % IN_PROJ_TN == 0
assert all(s % SUBLANES == 0 for s in IN_PROJ_TILE_STARTS) and P_AB % LANES == 0 and P_QC + XA_W == P_WIDTH

VMEM_LIMIT = 56 * 1024 * 1024


def _rms(x, g):
    return x * lax.rsqrt(jnp.mean(x * x, axis=-1, keepdims=True) + RMS_EPS) * g


def _dot(a, b):
    return jnp.dot(a, b, preferred_element_type=F32)


def _dot_nt(a, b):
    return lax.dot_general(a, b, (((1,), (1,)), ((), ())), preferred_element_type=F32)


def _dot_f32(a, b):
    return jnp.dot(a, b, preferred_element_type=F32, precision=lax.Precision.HIGHEST)


def _in_proj_kernel(starts_ref, x_ref, g_ref, wt_ref, o_ref, ab_ref, n_ref, *, ab_tile, ab_off):
    del starts_ref
    j = pl.program_id(1)

    @pl.when(j == 0)
    def _():
        n_ref[...] = _rms(x_ref[...], g_ref[...]).astype(BF16)

    acc = _dot_nt(n_ref[...], wt_ref[...].astype(BF16))
    o_ref[...] = acc.astype(o_ref.dtype)

    @pl.when(j == ab_tile)
    def _():
        ab_ref[...] = acc[:, ab_off:ab_off + LANES]


def _in_proj(x, g, wt, *, tm=1024, tn=IN_PROJ_TN):
    t, d = x.shape
    return pl.pallas_call(
        functools.partial(_in_proj_kernel, ab_tile=P_AB // tn, ab_off=P_AB % tn),
        out_shape=(jax.ShapeDtypeStruct((t, P_WIDTH), BF16), jax.ShapeDtypeStruct((t, LANES), F32)),
        grid_spec=pltpu.PrefetchScalarGridSpec(
            num_scalar_prefetch=1,
            grid=(t // tm, P_WIDTH // tn),
            in_specs=[
                pl.BlockSpec((tm, d), lambda i, j, starts: (i, 0)),
                pl.BlockSpec((1, d), lambda i, j, starts: (0, 0)),
                pl.BlockSpec((pl.Element(tn), pl.Element(d)),
                             lambda i, j, starts: (pl.multiple_of(starts[j], SUBLANES), 0)),
            ],
            out_specs=(pl.BlockSpec((tm, tn), lambda i, j, starts: (i, j)),
                       pl.BlockSpec((tm, LANES), lambda i, j, starts: (i, 0))),
            scratch_shapes=[pltpu.VMEM((tm, d), BF16)],
        ),
        compiler_params=pltpu.CompilerParams(
            dimension_semantics=("arbitrary", "arbitrary"), vmem_limit_bytes=VMEM_LIMIT),
        name="in_proj",
    )(jnp.asarray(IN_PROJ_TILE_STARTS, jnp.int32), x, g, wt)


def _mem_kv_kernel(m_ref, g_ref, w_ref, o_ref):
    o_ref[...] = _dot(_rms(m_ref[...], g_ref[...]).astype(BF16), w_ref[...].astype(BF16)).astype(o_ref.dtype)


def _mem_kv(mem, g, w):
    t, d = mem.shape
    n = w.shape[1]
    return pl.pallas_call(
        _mem_kv_kernel,
        out_shape=jax.ShapeDtypeStruct((t, n), BF16),
        grid=(t // N_MEM,),
        in_specs=[
            pl.BlockSpec((N_MEM, d), lambda i: (i, 0)),
            pl.BlockSpec((1, d), lambda i: (0, 0)),
            pl.BlockSpec((d, n), lambda i: (0, 0)),
        ],
        out_specs=pl.BlockSpec((N_MEM, n), lambda i: (i, 0)),
        compiler_params=pltpu.CompilerParams(
            dimension_semantics=("arbitrary",), vmem_limit_bytes=VMEM_LIMIT),
        name="mem_kv",
    )(mem, g, w)


def _swa_block(sinks_ref, q_ref, rows, k, v, first, o_ref):
    w = SWA_WINDOW
    hd = SWA_HEAD_DIM
    group = SWA_Q_HEADS // SWA_KV_HEADS
    ones = jnp.ones((2 * w, hd), BF16)
    qi = lax.broadcasted_iota(jnp.int32, (w, 2 * w), 0)
    kj = lax.broadcasted_iota(jnp.int32, (w, 2 * w), 1)
    valid = (kj > qi) & (kj <= qi + w)
    if first is not False:
        valid = valid & (kj >= jnp.where(first, w, 0))
    heads = range(SWA_Q_HEADS)
    kh = [k[:, hk * hd:(hk + 1) * hd] for hk in range(SWA_KV_HEADS)]
    vh = [jnp.concatenate([v[:, hk * hd:(hk + 1) * hd], ones], axis=1) for hk in range(SWA_KV_HEADS)]
    s = [_dot_nt(q_ref[rows, h * hd:(h + 1) * hd], kh[h // group]) for h in heads]
    s = [jnp.where(valid, s[h], -jnp.inf) for h in heads]
    m = [jnp.maximum(jnp.max(s[h], axis=-1, keepdims=True), sinks_ref[h]) for h in heads]
    p = [jnp.exp(s[h] - m[h]).astype(BF16) for h in heads]
    ov = [_dot(p[h], vh[h // group]) for h in heads]
    for h in heads:
        o = ov[h][:, :hd] / (ov[h][:, hd:hd + 1] + jnp.exp(sinks_ref[h] - m[h]))
        o_ref[rows, h * hd:(h + 1) * hd] = o.astype(o_ref.dtype)


XA_SUB = 128


def _xattn_rows(q_ref, mkv_ref, o_ref, col0):
    hd = XA_HEAD_DIM
    ones = jnp.ones((N_MEM, hd), BF16)
    units = [(slice(r, r + XA_SUB), h) for r in range(0, q_ref.shape[0], XA_SUB) for h in range(XA_HEADS)]
    mv = [jnp.concatenate([mkv_ref[:, XA_W + h * hd:XA_W + (h + 1) * hd], ones], axis=1) for h in range(XA_HEADS)]
    s = [_dot_nt(q_ref[rs, h * hd:(h + 1) * hd], mkv_ref[:, h * hd:(h + 1) * hd]) * (hd ** -0.5)
         for rs, h in units]
    e = [jnp.exp(si - jnp.max(si, axis=-1, keepdims=True)).astype(BF16) for si in s]
    oe = [_dot(ei, mv[h]) for ei, (_, h) in zip(e, units)]
    for (rs, h), oi in zip(units, oe):
        o_ref[rs, col0 + h * hd:col0 + (h + 1) * hd] = (oi[:, :hd] / oi[:, hd:hd + 1]).astype(o_ref.dtype)


GDN_TS = 256
GDN_QKV_BLOCK = 256
GDN_QKV_BLOCKS = 3 * GDN_W // GDN_QKV_BLOCK
assert P_QB % GDN_QKV_BLOCK == 0
CONV_PAD = SUBLANES


GDN_CAST_WEIGHTS = 4


def _gdn_kernel(*refs):
    qkv_refs = refs[:GDN_QKV_BLOCKS]
    refs = refs[GDN_QKV_BLOCKS:]
    z_ref, ab_ref, cw_ref, alog_ref, dtb_ref, nw_ref = refs[:6]
    wf_refs = refs[6:6 + GDN_CAST_WEIGHTS]
    y_ref = refs[6 + GDN_CAST_WEIGHTS]
    wb_refs = refs[7 + GDN_CAST_WEIGHTS:7 + 2 * GDN_CAST_WEIGHTS]
    state_ref, carry_ref, xpad_ref, cv_ref = refs[7 + 2 * GDN_CAST_WEIGHTS:]
    for wf_ref, wb_ref in zip(wf_refs, wb_refs):
        wb_ref[...] = wf_ref[...].astype(BF16)
    ts = GDN_TS
    c = GDN_CHUNK
    dh = GDN_HEAD_DIM
    gw = GDN_W
    heads = range(GDN_HEADS)
    chunks = [slice(i * c, (i + 1) * c) for i in range(ts // c)]

    @pl.when(pl.program_id(1) == 0)
    def _():
        state_ref[...] = jnp.zeros_like(state_ref)
        carry_ref[...] = jnp.zeros_like(carry_ref)

    xpad_ref[0:CONV_PAD, :] = carry_ref[...]
    for i, r in enumerate(qkv_refs):
        xpad_ref[CONV_PAD:, i * GDN_QKV_BLOCK:(i + 1) * GDN_QKV_BLOCK] = r[...].astype(F32)
    carry_ref[...] = xpad_ref[ts:ts + CONV_PAD, :]
    for cb in range(3 * gw // LANES):
        cs = slice(cb * LANES, (cb + 1) * LANES)
        acc = None
        for i in range(GDN_CONV):
            off = CONV_PAD - (GDN_CONV - 1) + i
            term = cw_ref[i:i + 1, cs] * xpad_ref[off:off + ts, cs]
            acc = term if acc is None else acc + term
        cv_ref[:, cs] = acc * jax.nn.sigmoid(acc)

    ab = ab_ref[...]
    g = -jnp.exp(alog_ref[...]) * jax.nn.softplus(ab + dtb_ref[...])
    beta_all = jax.nn.sigmoid(ab)
    ri = lax.broadcasted_iota(jnp.int32, (ts, ts), 0)
    ci = lax.broadcasted_iota(jnp.int32, (ts, ts), 1)
    same_chunk = (ri // c) == (ci // c)
    causal = same_chunk & (ri >= ci)
    strict = same_chunk & (ri > ci)
    gcum = _dot_f32(causal.astype(F32), g)
    gcum_t = gcum.T
    g_last = jnp.concatenate(
        [jnp.broadcast_to(gcum[rc.stop - 1:rc.stop], (c, LANES)) for rc in chunks], axis=0)
    k_scale = jnp.exp(g_last - gcum)
    eg_all = jnp.exp(gcum)

    q, k, v, gc, beta = [], [], [], [], []
    for h in heads:
        qh = cv_ref[:, h * dh:(h + 1) * dh]
        kh = cv_ref[:, gw + h * dh:gw + (h + 1) * dh]
        q.append(qh * lax.rsqrt(jnp.sum(qh * qh, axis=-1, keepdims=True) + L2_EPS) * (dh ** -0.5))
        k.append(kh * lax.rsqrt(jnp.sum(kh * kh, axis=-1, keepdims=True) + L2_EPS))
        v.append(cv_ref[:, 2 * gw + h * dh:2 * gw + (h + 1) * dh])
        gc.append(gcum[:, h:h + 1])
        beta.append(beta_all[:, GDN_HEADS + h:GDN_HEADS + h + 1])
    qkk = [_dot_nt(jnp.concatenate([q[h], k[h]], axis=0).astype(BF16), k[h].astype(BF16)) for h in heads]
    decay = [jnp.exp(jnp.where(causal, gc[h] - gcum_t[h:h + 1, :], -jnp.inf)) for h in heads]
    qk = [qkk[h][:ts] * decay[h] for h in heads]
    m = [-jnp.where(strict, beta[h] * qkk[h][ts:] * decay[h], 0.0) for h in heads]
    sol = [jnp.concatenate([v[h] * beta[h], k[h] * (beta[h] * eg_all[:, h:h + 1])], axis=-1) for h in heads]
    n_rounds = c.bit_length() - 1
    for r in range(n_rounds):
        last = r + 1 == n_rounds
        mb = [m[h].astype(BF16) for h in heads]
        rhs = [sol[h].astype(BF16) if last else jnp.concatenate([sol[h], m[h]], axis=-1).astype(BF16)
               for h in heads]
        prod = [_dot(mb[h], rhs[h]) for h in heads]
        sol = [sol[h] + prod[h][:, :2 * dh] for h in heads]
        if not last:
            m = [prod[h][:, 2 * dh:] for h in heads]
    k_dec = [k[h] * k_scale[:, h:h + 1] for h in heads]
    q_dec = [q[h] * eg_all[:, h:h + 1] for h in heads]

    x = {}
    for ic, rc in enumerate(chunks):
        for h in heads:
            lhs = jnp.concatenate([k_dec[h][rc].T, qk[h][rc, rc]], axis=0).astype(BF16)
            x[ic, h] = _dot(lhs, sol[h][rc].astype(BF16))

    for ic, rc in enumerate(chunks):
        for h in heads:
            xs = x[ic, h]
            lhs = jnp.concatenate([-xs[:dh, dh:], q_dec[h][rc] - xs[dh:, dh:]], axis=0).astype(BF16)
            s = state_ref[h]
            y = _dot(lhs, s.astype(BF16))
            state_ref[h] = s * eg_all[rc.stop - 1:rc.stop, h:h + 1] + y[:dh] + xs[:dh, :dh]
            o = _rms(y[dh:] + xs[dh:, :dh], nw_ref[...])
            hs = slice(h * dh, (h + 1) * dh)
            z = z_ref[rc, hs].astype(F32)
            y_ref[rc, hs] = (o * (z * jax.nn.sigmoid(z))).astype(y_ref.dtype)


def _gdn(p, ab, conv_w, alog_row, dtb_row, norm_w, cast_weights, batch, seq):
    t = p.shape[0]
    ts = GDN_TS
    nj = seq // ts
    gw = GDN_W
    n_steps = batch * nj
    assert len(cast_weights) == GDN_CAST_WEIGHTS

    def rows(col_blk):
        return lambda b, j: (b * nj + j, col_blk)

    def row_slab(wf):
        r, c = wf.shape
        slab = max(r // n_steps, BF16_SUBLANES)
        assert r % slab == 0 and n_steps % (r // slab) == 0
        reuse = n_steps // (r // slab)
        return pl.BlockSpec((slab, c), lambda b, j: ((b * nj + j) // reuse, 0))

    w_specs = [row_slab(wf) for wf in cast_weights]
    return pl.pallas_call(
        _gdn_kernel,
        out_shape=(jax.ShapeDtypeStruct((t, gw), BF16), *[jax.ShapeDtypeStruct(wf.shape, BF16) for wf in cast_weights]),
        grid=(batch, nj),
        in_specs=[
            *[pl.BlockSpec((ts, GDN_QKV_BLOCK), rows(P_QB // GDN_QKV_BLOCK + i)) for i in range(GDN_QKV_BLOCKS)],
            pl.BlockSpec((ts, gw), rows(P_Z // gw)),
            pl.BlockSpec((ts, LANES), rows(0)),
            pl.BlockSpec((GDN_CONV, 3 * gw), lambda b, j: (0, 0)),
            pl.BlockSpec((1, LANES), lambda b, j: (0, 0)),
            pl.BlockSpec((1, LANES), lambda b, j: (0, 0)),
            pl.BlockSpec((1, GDN_HEAD_DIM), lambda b, j: (0, 0)),
            *w_specs,
        ],
        out_specs=(pl.BlockSpec((ts, gw), rows(0)), *w_specs),
        scratch_shapes=[
            pltpu.VMEM((GDN_HEADS, GDN_HEAD_DIM, GDN_HEAD_DIM), F32),
            pltpu.VMEM((CONV_PAD, 3 * gw), F32),
            pltpu.VMEM((ts + CONV_PAD, 3 * gw), F32),
            pltpu.VMEM((ts, 3 * gw), F32),
        ],
        compiler_params=pltpu.CompilerParams(dimension_semantics=("arbitrary", "arbitrary")),
        name="gdn",
    )(*([p] * GDN_QKV_BLOCKS), p, ab, conv_w, alog_row, dtb_row, norm_w, *cast_weights)


def _mix_kernel(sinks_ref, q_ref, kv_ref, kvp_ref, qc_ref, mkv_ref, x_ref, gate_ref, yb_ref,
                wa_ref, wb_ref, wc_ref, wo_ref, w1_ref, w2_ref, h_ref, w1b_ref, w2b_ref, y0_ref, y1_ref, *,
                tiles_per_seq, n_tiles):
    i = pl.program_id(0)
    w = SWA_WINDOW
    hd = SWA_HEAD_DIM

    @pl.when(i == 0)
    def _():
        y0_ref[...] = jnp.zeros_like(y0_ref)

    def step(yprev_ref, ynext_ref):
        w1b_ref[...] = w1_ref[...].astype(BF16)
        w2b_ref[...] = w2_ref[...].astype(BF16)

        first = (jnp.minimum(i, n_tiles - 1) % tiles_per_seq) == 0
        kvw = SWA_KV_W
        k_all = (jnp.concatenate([kvp_ref[:, :kvw], kv_ref[:, :kvw]], axis=0).astype(F32)
                 * (hd ** -0.5)).astype(BF16)
        v_all = jnp.concatenate([kvp_ref[:, kvw:], kv_ref[:, kvw:]], axis=0)
        for b in range(q_ref.shape[0] // w):
            _swa_block(sinks_ref, q_ref, slice(b * w, (b + 1) * w), k_all[b * w:(b + 2) * w],
                       v_all[b * w:(b + 2) * w], first if b == 0 else False, ynext_ref)
        _xattn_rows(qc_ref, mkv_ref, ynext_ref, SWA_Q_W)

        d = x_ref.shape[1]
        gate = [jax.nn.sigmoid(gate_ref[:, b * d:(b + 1) * d].astype(F32)) for b in range(3)]
        merged = gate[0] * _dot(yprev_ref[:, :SWA_Q_W], wa_ref[...])
        merged += gate[1] * _dot(yb_ref[...], wb_ref[...])
        merged += gate[2] * _dot(yprev_ref[:, SWA_Q_W:], wc_ref[...])
        h_ref[...] = x_ref[...] + _dot(merged.astype(BF16), wo_ref[...])

    @pl.when(i % 2 == 0)
    def _():
        step(y0_ref, y1_ref)

    @pl.when(i % 2 == 1)
    def _():
        step(y1_ref, y0_ref)


def _mix(x, p, mkv, yb, sinks, wa, wb, wc, wo, w1, w2, seq, *, tm=256):
    t, d = x.shape
    w = SWA_WINDOW
    n_tiles = t // tm
    tiles_per_seq = seq // tm
    kv_w = 2 * SWA_KV_W
    assert P_VA == P_KA + SWA_KV_W and P_KA % kv_w == 0

    def att(i):
        return jnp.minimum(i, n_tiles - 1)

    def mm(i):
        return jnp.maximum(i - 1, 0)

    def resident(shape):
        return pl.BlockSpec(shape, lambda i: (0, 0), pipeline_mode=pl.Buffered(1))

    def row_slab(wf):
        r, c = wf.shape
        assert r % (n_tiles * BF16_SUBLANES) == 0
        return pl.BlockSpec((r // n_tiles, c), lambda i: (att(i), 0))

    return pl.pallas_call(
        functools.partial(_mix_kernel, tiles_per_seq=tiles_per_seq, n_tiles=n_tiles),
        out_shape=(jax.ShapeDtypeStruct((t, d), F32), jax.ShapeDtypeStruct(w1.shape, BF16),
                   jax.ShapeDtypeStruct(w2.shape, BF16)),
        grid=(n_tiles + 1,),
        in_specs=[
            pl.BlockSpec(memory_space=pltpu.SMEM),
            pl.BlockSpec((tm, SWA_Q_W), lambda i: (att(i), P_QA // SWA_Q_W)),
            pl.BlockSpec((tm, kv_w), lambda i: (att(i), P_KA // kv_w)),
            pl.BlockSpec((w, kv_w), lambda i: (jnp.maximum(att(i) * (tm // w) - 1, 0), P_KA // kv_w)),
            pl.BlockSpec((tm, XA_W), lambda i: (att(i), P_QC // XA_W)),
            pl.BlockSpec((N_MEM, 2 * XA_W), lambda i: (att(i) // tiles_per_seq, 0)),
            pl.BlockSpec((tm, d), lambda i: (mm(i), 0)),
            pl.BlockSpec((tm, 3 * d), lambda i: (mm(i), P_GATE // (3 * d))),
            pl.BlockSpec((tm, GDN_W), lambda i: (mm(i), 0)),
            resident((SWA_Q_W, d)),
            resident((GDN_W, d)),
            resident((XA_W, d)),
            resident((d, d)),
            row_slab(w1),
            row_slab(w2),
        ],
        out_specs=(pl.BlockSpec((tm, d), lambda i: (mm(i), 0)), row_slab(w1), row_slab(w2)),
        scratch_shapes=[pltpu.VMEM((tm, SWA_Q_W + XA_W), BF16), pltpu.VMEM((tm, SWA_Q_W + XA_W), BF16)],
        compiler_params=pltpu.CompilerParams(
            dimension_semantics=("arbitrary",), vmem_limit_bytes=VMEM_LIMIT),
        name="mix",
    )(sinks, p, p, p, p, mkv, x, p, yb, wa, wb, wc, wo, w1, w2)


def _mlp_kernel(h_ref, g_ref, w1_ref, w2_ref, gf_ref, o_ref, n_ref, acc_ref, *, final_norm):
    j = pl.program_id(1)

    @pl.when(j == 0)
    def _():
        h = h_ref[...]
        n_ref[...] = _rms(h, g_ref[...]).astype(BF16)
        acc_ref[...] = h

    u = _dot(n_ref[...], w1_ref[...])
    acc_ref[...] += _dot(jnp.square(jnp.maximum(u, 0.0)).astype(BF16), w2_ref[...])

    @pl.when(j == pl.num_programs(1) - 1)
    def _():
        h = acc_ref[...]
        o_ref[...] = _rms(h, gf_ref[...]) if final_norm else h


def _mlp(h, g, w1, w2, g_final, *, final_norm, tm=512, tf=1024):
    t, d = h.shape
    f = w1.shape[1]
    return pl.pallas_call(
        functools.partial(_mlp_kernel, final_norm=final_norm),
        out_shape=jax.ShapeDtypeStruct((t, d), F32),
        grid=(t // tm, f // tf),
        in_specs=[
            pl.BlockSpec((tm, d), lambda i, j: (i, 0)),
            pl.BlockSpec((1, d), lambda i, j: (0, 0)),
            pl.BlockSpec((d, tf), lambda i, j: (0, j)),
            pl.BlockSpec((tf, d), lambda i, j: (j, 0)),
            pl.BlockSpec((1, d), lambda i, j: (0, 0)),
        ],
        out_specs=pl.BlockSpec((tm, d), lambda i, j: (i, 0)),
        scratch_shapes=[pltpu.VMEM((tm, d), BF16), pltpu.VMEM((tm, d), F32)],
        compiler_params=pltpu.CompilerParams(
            dimension_semantics=("arbitrary", "arbitrary"), vmem_limit_bytes=VMEM_LIMIT),
        name="mlp",
    )(h, g, w1, w2, g_final)


def _lane_row(v):
    return jnp.zeros((1, LANES), F32).at[0, :v.shape[0]].set(v.astype(F32))


def kernel(x, mem, g_mix, w_in, sinks, conv_w, a_log, dt_bias, gdn_norm_w, g_mem, w_mem_kv, w_swa_up,
           w_gdn_up, w_xa_up, w_out, g_mlp, w_mlp_in, w_mlp_out, g_final):
    batch, seq, d = x.shape
    depth = w_in.shape[0]
    h = x.reshape(batch * seq, d)
    mem2 = mem.reshape(batch * N_MEM, d)
    for l in range(depth):
        p, ab = _in_proj(h, g_mix[l][None], jnp.swapaxes(w_in[l], 0, 1))
        mkv = _mem_kv(mem2, g_mem[l][None], w_mem_kv[l])
        yb, wo, wa, wb, wc = _gdn(p, ab, conv_w[l], _lane_row(a_log[l]), _lane_row(dt_bias[l]),
                                  gdn_norm_w[l][None], (w_out[l], w_swa_up[l], w_gdn_up[l], w_xa_up[l]),
                                  batch, seq)
        h, w1, w2 = _mix(h, p, mkv, yb, sinks[l], wa, wb, wc, wo, w_mlp_in[l], w_mlp_out[l], seq)
        h = _mlp(h, g_mlp[l][None], w1, w2, g_final[None], final_norm=(l == depth - 1))
    return h.reshape(batch, seq, d)
```

```python
import functools

import jax
import jax.numpy as jnp
from jax import lax
from jax.experimental import pallas as pl
from jax.experimental.pallas import tpu as pltpu

F32 = jnp.float32
BF16 = jnp.bfloat16

D_MODEL = 2048
SWA_Q_HEADS = 16
SWA_KV_HEADS = 2
SWA_HEAD_DIM = 64
SWA_WINDOW = 128
GDN_HEADS = 4
GDN_HEAD_DIM = 128
GDN_CONV = 4
GDN_CHUNK = 64
N_MEM = 256
XA_HEADS = 4
XA_HEAD_DIM = 128
D_FF = 4 * D_MODEL
RMS_EPS = 1e-6
L2_EPS = 1e-6

SWA_Q_W = SWA_Q_HEADS * SWA_HEAD_DIM
SWA_KV_W = SWA_KV_HEADS * SWA_HEAD_DIM
GDN_W = GDN_HEADS * GDN_HEAD_DIM
XA_W = XA_HEADS * XA_HEAD_DIM

LANES = 128
SUBLANES = 8
BF16_SUBLANES = 16

AB_W = 2 * GDN_HEADS

R_QA = 0
R_KA = R_QA + SWA_Q_W
R_VA = R_KA + SWA_KV_W
R_QB = R_VA + SWA_KV_W
R_AB = R_QB + 3 * GDN_W
R_Z = R_AB + AB_W
R_QC = R_Z + GDN_W
R_GATE = R_QC + XA_W
R_END = R_GATE + 3 * D_MODEL

IN_PROJ_TN = 1024
IN_PROJ_TILE_STARTS = (
    tuple(R_GATE + k * IN_PROJ_TN for k in range(3 * D_MODEL // IN_PROJ_TN))
    + tuple(range(0, R_Z, IN_PROJ_TN))
    + tuple(R_Z + k * IN_PROJ_TN for k in range((R_GATE - R_Z) // IN_PROJ_TN)))
P_WIDTH = len(IN_PROJ_TILE_STARTS) * IN_PROJ_TN
P_GATE = 0
P_FRONT = 3 * D_MODEL
P_QA = P_FRONT + R_QA
P_KA = P_FRONT + R_KA
P_VA = P_FRONT + R_VA
P_QB = P_FRONT + R_QB
P_AB = P_FRONT + R_AB
P_Z = P_FRONT + -(-R_Z // IN_PROJ_TN) * IN_PROJ_TN
P_QC = P_Z + GDN_W
assert (R_GATE - R_Z) % IN_PROJ_TN == 0 and (3 * D_MODEL) % IN_PROJ_TN == 0
assert all(s % SUBLANES == 0 for s in IN_PROJ_TILE_STARTS) and P_AB % LANES == 0 and P_QC + XA_W == P_WIDTH

VMEM_LIMIT = 56 * 1024 * 1024
IN_PROJ_VMEM_LIMIT = 60 * 1024 * 1024


def _rms(x, g):
    return x * lax.rsqrt(jnp.mean(x * x, axis=-1, keepdims=True) + RMS_EPS) * g


def _dot(a, b):
    return jnp.dot(a, b, preferred_element_type=F32)


def _dot_nt(a, b):
    return lax.dot_general(a, b, (((1,), (1,)), ((), ())), preferred_element_type=F32)


def _dot_f32(a, b):
    return jnp.dot(a, b, preferred_element_type=F32, precision=lax.Precision.HIGHEST)


W_RING = 3


def _in_proj_kernel(starts_ref, x_ref, g_ref, wt_hbm, o_ref, ab_ref, n_ref, wbuf_ref, sem_ref, *, ab_tile, ab_off):
    j = pl.program_id(1)
    n_col = pl.num_programs(1)
    n_steps = pl.num_programs(0) * n_col
    s = pl.program_id(0) * n_col + j
    tn = o_ref.shape[1]

    def tile_copy(step):
        start = pl.multiple_of(starts_ref[step % n_col], SUBLANES)
        slot = step % W_RING
        return pltpu.make_async_copy(wt_hbm.at[pl.ds(start, tn), :], wbuf_ref.at[slot], sem_ref.at[slot])

    @pl.when(s == 0)
    def _():
        for first in range(W_RING - 1):
            tile_copy(first).start()

    @pl.when(s + W_RING - 1 < n_steps)
    def _():
        tile_copy(s + W_RING - 1).start()

    @pl.when(j == 0)
    def _():
        n_ref[...] = _rms(x_ref[...], g_ref[...]).astype(BF16)

    tile_copy(s).wait()
    acc = _dot_nt(n_ref[...], wbuf_ref[s % W_RING].astype(BF16))
    o_ref[...] = acc.astype(o_ref.dtype)

    @pl.when(j == ab_tile)
    def _():
        ab_ref[...] = acc[:, ab_off:ab_off + LANES]


def _in_proj(x, g, wt, *, tm=1024, tn=IN_PROJ_TN):
    t, d = x.shape
    return pl.pallas_call(
        functools.partial(_in_proj_kernel, ab_tile=P_AB // tn, ab_off=P_AB % tn),
        out_shape=(jax.ShapeDtypeStruct((t, P_WIDTH), BF16), jax.ShapeDtypeStruct((t, LANES), F32)),
        grid_spec=pltpu.PrefetchScalarGridSpec(
            num_scalar_prefetch=1,
            grid=(t // tm, P_WIDTH // tn),
            in_specs=[
                pl.BlockSpec((tm, d), lambda i, j, starts: (i, 0)),
                pl.BlockSpec((1, d), lambda i, j, starts: (0, 0)),
                pl.BlockSpec(memory_space=pl.ANY),
            ],
            out_specs=(pl.BlockSpec((tm, tn), lambda i, j, starts: (i, j)),
                       pl.BlockSpec((tm, LANES), lambda i, j, starts: (i, 0))),
            scratch_shapes=[pltpu.VMEM((tm, d), BF16), pltpu.VMEM((W_RING, tn, d), F32),
                            pltpu.SemaphoreType.DMA((W_RING,))],
        ),
        compiler_params=pltpu.CompilerParams(
            dimension_semantics=("arbitrary", "arbitrary"), vmem_limit_bytes=IN_PROJ_VMEM_LIMIT),
        name="in_proj",
    )(jnp.asarray(IN_PROJ_TILE_STARTS, jnp.int32), x, g, wt)


def _mem_kv_kernel(m_ref, g_ref, w_ref, o_ref):
    o_ref[...] = _dot(_rms(m_ref[...], g_ref[...]).astype(BF16), w_ref[...].astype(BF16)).astype(o_ref.dtype)


def _mem_kv(mem, g, w):
    t, d = mem.shape
    n = w.shape[1]
    return pl.pallas_call(
        _mem_kv_kernel,
        out_shape=jax.ShapeDtypeStruct((t, n), BF16),
        grid=(t // N_MEM,),
        in_specs=[
            pl.BlockSpec((N_MEM, d), lambda i: (i, 0)),
            pl.BlockSpec((1, d), lambda i: (0, 0)),
            pl.BlockSpec((d, n), lambda i: (0, 0)),
        ],
        out_specs=pl.BlockSpec((N_MEM, n), lambda i: (i, 0)),
        compiler_params=pltpu.CompilerParams(
            dimension_semantics=("arbitrary",), vmem_limit_bytes=VMEM_LIMIT),
        name="mem_kv",
    )(mem, g, w)


def _swa_block(sinks_ref, q_ref, rows, k, v, first, o_ref):
    w = SWA_WINDOW
    hd = SWA_HEAD_DIM
    group = SWA_Q_HEADS // SWA_KV_HEADS
    ones = jnp.ones((2 * w, hd), BF16)
    qi = lax.broadcasted_iota(jnp.int32, (w, 2 * w), 0)
    kj = lax.broadcasted_iota(jnp.int32, (w, 2 * w), 1)
    valid = (kj > qi) & (kj <= qi + w)
    if first is not False:
        valid = valid & (kj >= jnp.where(first, w, 0))
    heads = range(SWA_Q_HEADS)
    kh = [k[:, hk * hd:(hk + 1) * hd] for hk in range(SWA_KV_HEADS)]
    vh = [jnp.concatenate([v[:, hk * hd:(hk + 1) * hd], ones], axis=1) for hk in range(SWA_KV_HEADS)]
    s = [_dot_nt(q_ref[rows, h * hd:(h + 1) * hd], kh[h // group]) for h in heads]
    s = [jnp.where(valid, s[h], -jnp.inf) for h in heads]
    m = [jnp.maximum(jnp.max(s[h], axis=-1, keepdims=True), sinks_ref[h]) for h in heads]
    p = [jnp.exp(s[h] - m[h]).astype(BF16) for h in heads]
    ov = [_dot(p[h], vh[h // group]) for h in heads]
    for h in heads:
        o = ov[h][:, :hd] / (ov[h][:, hd:hd + 1] + jnp.exp(sinks_ref[h] - m[h]))
        o_ref[rows, h * hd:(h + 1) * hd] = o.astype(o_ref.dtype)


XA_SUB = 128


def _xattn_rows(q_ref, mkv_ref, o_ref, col0):
    hd = XA_HEAD_DIM
    ones = jnp.ones((N_MEM, hd), BF16)
    units = [(slice(r, r + XA_SUB), h) for r in range(0, q_ref.shape[0], XA_SUB) for h in range(XA_HEADS)]
    mv = [jnp.concatenate([mkv_ref[:, XA_W + h * hd:XA_W + (h + 1) * hd], ones], axis=1) for h in range(XA_HEADS)]
    s = [_dot_nt(q_ref[rs, h * hd:(h + 1) * hd], mkv_ref[:, h * hd:(h + 1) * hd]) * (hd ** -0.5)
         for rs, h in units]
    e = [jnp.exp(si - jnp.max(si, axis=-1, keepdims=True)).astype(BF16) for si in s]
    oe = [_dot(ei, mv[h]) for ei, (_, h) in zip(e, units)]
    for (rs, h), oi in zip(units, oe):
        o_ref[rs, col0 + h * hd:col0 + (h + 1) * hd] = (oi[:, :hd] / oi[:, hd:hd + 1]).astype(o_ref.dtype)


GDN_TS = 256
GDN_QKV_BLOCK = 256
GDN_QKV_BLOCKS = 3 * GDN_W // GDN_QKV_BLOCK
assert P_QB % GDN_QKV_BLOCK == 0
CONV_PAD = SUBLANES


GDN_CAST_WEIGHTS = 4


def _gdn_kernel(*refs):
    qkv_refs = refs[:GDN_QKV_BLOCKS]
    refs = refs[GDN_QKV_BLOCKS:]
    z_ref, ab_ref, cw_ref, alog_ref, dtb_ref, nw_ref = refs[:6]
    wf_refs = refs[6:6 + GDN_CAST_WEIGHTS]
    y_ref = refs[6 + GDN_CAST_WEIGHTS]
    wb_refs = refs[7 + GDN_CAST_WEIGHTS:7 + 2 * GDN_CAST_WEIGHTS]
    state_ref, carry_ref, xpad_ref, cv_ref = refs[7 + 2 * GDN_CAST_WEIGHTS:]
    for wf_ref, wb_ref in zip(wf_refs, wb_refs):
        wb_ref[...] = wf_ref[...].astype(BF16)
    ts = GDN_TS
    c = GDN_CHUNK
    dh = GDN_HEAD_DIM
    gw = GDN_W
    heads = range(GDN_HEADS)
    chunks = [slice(i * c, (i + 1) * c) for i in range(ts // c)]

    @pl.when(pl.program_id(1) == 0)
    def _():
        state_ref[...] = jnp.zeros_like(state_ref)
        carry_ref[...] = jnp.zeros_like(carry_ref)

    xpad_ref[0:CONV_PAD, :] = carry_ref[...]
    for i, r in enumerate(qkv_refs):
        xpad_ref[CONV_PAD:, i * GDN_QKV_BLOCK:(i + 1) * GDN_QKV_BLOCK] = r[...].astype(F32)
    carry_ref[...] = xpad_ref[ts:ts + CONV_PAD, :]
    for cb in range(3 * gw // LANES):
        cs = slice(cb * LANES, (cb + 1) * LANES)
        acc = None
        for i in range(GDN_CONV):
            off = CONV_PAD - (GDN_CONV - 1) + i
            term = cw_ref[i:i + 1, cs] * xpad_ref[off:off + ts, cs]
            acc = term if acc is None else acc + term
        cv_ref[:, cs] = acc * jax.nn.sigmoid(acc)

    ab = ab_ref[...]
    g = -jnp.exp(alog_ref[...]) * jax.nn.softplus(ab + dtb_ref[...])
    beta_all = jax.nn.sigmoid(ab)
    ri = lax.broadcasted_iota(jnp.int32, (ts, ts), 0)
    ci = lax.broadcasted_iota(jnp.int32, (ts, ts), 1)
    same_chunk = (ri // c) == (ci // c)
    causal = same_chunk & (ri >= ci)
    strict = same_chunk & (ri > ci)
    gcum = _dot_f32(causal.astype(F32), g)
    gcum_t = gcum.T
    g_last = jnp.concatenate(
        [jnp.broadcast_to(gcum[rc.stop - 1:rc.stop], (c, LANES)) for rc in chunks], axis=0)
    k_scale = jnp.exp(g_last - gcum)
    eg_all = jnp.exp(gcum)

    q, k, v, gc, beta = [], [], [], [], []
    for h in heads:
        qh = cv_ref[:, h * dh:(h + 1) * dh]
        kh = cv_ref[:, gw + h * dh:gw + (h + 1) * dh]
        q.append(qh * lax.rsqrt(jnp.sum(qh * qh, axis=-1, keepdims=True) + L2_EPS) * (dh ** -0.5))
        k.append(kh * lax.rsqrt(jnp.sum(kh * kh, axis=-1, keepdims=True) + L2_EPS))
        v.append(cv_ref[:, 2 * gw + h * dh:2 * gw + (h + 1) * dh])
        gc.append(gcum[:, h:h + 1])
        beta.append(beta_all[:, GDN_HEADS + h:GDN_HEADS + h + 1])
    qkk = [_dot_nt(jnp.concatenate([q[h], k[h]], axis=0).astype(BF16), k[h].astype(BF16)) for h in heads]
    decay = [jnp.exp(jnp.where(causal, gc[h] - gcum_t[h:h + 1, :], -jnp.inf)) for h in heads]
    qk = [qkk[h][:ts] * decay[h] for h in heads]
    m = [-jnp.where(strict, beta[h] * qkk[h][ts:] * decay[h], 0.0) for h in heads]
    sol = [jnp.concatenate([v[h] * beta[h], k[h] * (beta[h] * eg_all[:, h:h + 1])], axis=-1) for h in heads]
    n_rounds = c.bit_length() - 1
    for r in range(n_rounds):
        last = r + 1 == n_rounds
        mb = [m[h].astype(BF16) for h in heads]
        rhs = [sol[h].astype(BF16) if last else jnp.concatenate([sol[h], m[h]], axis=-1).astype(BF16)
               for h in heads]
        prod = [_dot(mb[h], rhs[h]) for h in heads]
        sol = [sol[h] + prod[h][:, :2 * dh] for h in heads]
        if not last:
            m = [prod[h][:, 2 * dh:] for h in heads]
    k_dec = [k[h] * k_scale[:, h:h + 1] for h in heads]
    q_dec = [q[h] * eg_all[:, h:h + 1] for h in heads]

    x = {}
    for ic, rc in enumerate(chunks):
        for h in heads:
            lhs = jnp.concatenate([k_dec[h][rc].T, qk[h][rc, rc]], axis=0).astype(BF16)
            x[ic, h] = _dot(lhs, sol[h][rc].astype(BF16))

    for ic, rc in enumerate(chunks):
        for h in heads:
            xs = x[ic, h]
            lhs = jnp.concatenate([-xs[:dh, dh:], q_dec[h][rc] - xs[dh:, dh:]], axis=0).astype(BF16)
            s = state_ref[h]
            y = _dot(lhs, s.astype(BF16))
            state_ref[h] = s * eg_all[rc.stop - 1:rc.stop, h:h + 1] + y[:dh] + xs[:dh, :dh]
            o = _rms(y[dh:] + xs[dh:, :dh], nw_ref[...])
            hs = slice(h * dh, (h + 1) * dh)
            z = z_ref[rc, hs].astype(F32)
            y_ref[rc, hs] = (o * (z * jax.nn.sigmoid(z))).astype(y_ref.dtype)


def _gdn(p, ab, conv_w, alog_row, dtb_row, norm_w, cast_weights, batch, seq):
    t = p.shape[0]
    ts = GDN_TS
    nj = seq // ts
    gw = GDN_W
    n_steps = batch * nj
    assert len(cast_weights) == GDN_CAST_WEIGHTS

    def rows(col_blk):
        return lambda b, j: (b * nj + j, col_blk)

    def row_slab(wf):
        r, c = wf.shape
        slab = max(r // n_steps, BF16_SUBLANES)
        assert r % slab == 0 and n_steps % (r // slab) == 0
        reuse = n_steps // (r // slab)
        return pl.BlockSpec((slab, c), lambda b, j: ((b * nj + j) // reuse, 0))

    w_specs = [row_slab(wf) for wf in cast_weights]
    return pl.pallas_call(
        _gdn_kernel,
        out_shape=(jax.ShapeDtypeStruct((t, gw), BF16), *[jax.ShapeDtypeStruct(wf.shape, BF16) for wf in cast_weights]),
        grid=(batch, nj),
        in_specs=[
            *[pl.BlockSpec((ts, GDN_QKV_BLOCK), rows(P_QB // GDN_QKV_BLOCK + i)) for i in range(GDN_QKV_BLOCKS)],
            pl.BlockSpec((ts, gw), rows(P_Z // gw)),
            pl.BlockSpec((ts, LANES), rows(0)),
            pl.BlockSpec((GDN_CONV, 3 * gw), lambda b, j: (0, 0)),
            pl.BlockSpec((1, LANES), lambda b, j: (0, 0)),
            pl.BlockSpec((1, LANES), lambda b, j: (0, 0)),
            pl.BlockSpec((1, GDN_HEAD_DIM), lambda b, j: (0, 0)),
            *w_specs,
        ],
        out_specs=(pl.BlockSpec((ts, gw), rows(0)), *w_specs),
        scratch_shapes=[
            pltpu.VMEM((GDN_HEADS, GDN_HEAD_DIM, GDN_HEAD_DIM), F32),
            pltpu.VMEM((CONV_PAD, 3 * gw), F32),
            pltpu.VMEM((ts + CONV_PAD, 3 * gw), F32),
            pltpu.VMEM((ts, 3 * gw), F32),
        ],
        compiler_params=pltpu.CompilerParams(dimension_semantics=("arbitrary", "arbitrary")),
        name="gdn",
    )(*([p] * GDN_QKV_BLOCKS), p, ab, conv_w, alog_row, dtb_row, norm_w, *cast_weights)


def _mix_kernel(sinks_ref, q_ref, kv_ref, kvp_ref, qc_ref, mkv_ref, x_ref, gate_ref, yb_ref,
                wa_ref, wb_ref, wc_ref, wo_ref, w1_ref, w2_ref, h_ref, w1b_ref, w2b_ref, y0_ref, y1_ref, *,
                tiles_per_seq, n_tiles):
    i = pl.program_id(0)
    w = SWA_WINDOW
    hd = SWA_HEAD_DIM

    @pl.when(i == 0)
    def _():
        y0_ref[...] = jnp.zeros_like(y0_ref)

    def step(yprev_ref, ynext_ref):
        w1b_ref[...] = w1_ref[...].astype(BF16)
        w2b_ref[...] = w2_ref[...].astype(BF16)

        first = (jnp.minimum(i, n_tiles - 1) % tiles_per_seq) == 0
        kvw = SWA_KV_W
        k_all = (jnp.concatenate([kvp_ref[:, :kvw], kv_ref[:, :kvw]], axis=0).astype(F32)
                 * (hd ** -0.5)).astype(BF16)
        v_all = jnp.concatenate([kvp_ref[:, kvw:], kv_ref[:, kvw:]], axis=0)
        for b in range(q_ref.shape[0] // w):
            _swa_block(sinks_ref, q_ref, slice(b * w, (b + 1) * w), k_all[b * w:(b + 2) * w],
                       v_all[b * w:(b + 2) * w], first if b == 0 else False, ynext_ref)
        _xattn_rows(qc_ref, mkv_ref, ynext_ref, SWA_Q_W)

        d = x_ref.shape[1]
        gate = [jax.nn.sigmoid(gate_ref[:, b * d:(b + 1) * d].astype(F32)) for b in range(3)]
        merged = gate[0] * _dot(yprev_ref[:, :SWA_Q_W], wa_ref[...])
        merged += gate[1] * _dot(yb_ref[...], wb_ref[...])
        merged += gate[2] * _dot(yprev_ref[:, SWA_Q_W:], wc_ref[...])
        h_ref[...] = x_ref[...] + _dot(merged.astype(BF16), wo_ref[...])

    @pl.when(i % 2 == 0)
    def _():
        step(y0_ref, y1_ref)

    @pl.when(i % 2 == 1)
    def _():
        step(y1_ref, y0_ref)


def _mix(x, p, mkv, yb, sinks, wa, wb, wc, wo, w1, w2, seq, *, tm=256):
    t, d = x.shape
    w = SWA_WINDOW
    n_tiles = t // tm
    tiles_per_seq = seq // tm
    kv_w = 2 * SWA_KV_W
    assert P_VA == P_KA + SWA_KV_W and P_KA % kv_w == 0

    def att(i):
        return jnp.minimum(i, n_tiles - 1)

    def mm(i):
        return jnp.maximum(i - 1, 0)

    def resident(shape):
        return pl.BlockSpec(shape, lambda i: (0, 0), pipeline_mode=pl.Buffered(1))

    def row_slab(wf):
        r, c = wf.shape
        assert r % (n_tiles * BF16_SUBLANES) == 0
        return pl.BlockSpec((r // n_tiles, c), lambda i: (att(i), 0))

    return pl.pallas_call(
        functools.partial(_mix_kernel, tiles_per_seq=tiles_per_seq, n_tiles=n_tiles),
        out_shape=(jax.ShapeDtypeStruct((t, d), F32), jax.ShapeDtypeStruct(w1.shape, BF16),
                   jax.ShapeDtypeStruct(w2.shape, BF16)),
        grid=(n_tiles + 1,),
        in_specs=[
            pl.BlockSpec(memory_space=pltpu.SMEM),
            pl.BlockSpec((tm, SWA_Q_W), lambda i: (att(i), P_QA // SWA_Q_W)),
            pl.BlockSpec((tm, kv_w), lambda i: (att(i), P_KA // kv_w)),
            pl.BlockSpec((w, kv_w), lambda i: (jnp.maximum(att(i) * (tm // w) - 1, 0), P_KA // kv_w)),
            pl.BlockSpec((tm, XA_W), lambda i: (att(i), P_QC // XA_W)),
            pl.BlockSpec((N_MEM, 2 * XA_W), lambda i: (att(i) // tiles_per_seq, 0)),
            pl.BlockSpec((tm, d), lambda i: (mm(i), 0)),
            pl.BlockSpec((tm, 3 * d), lambda i: (mm(i), P_GATE // (3 * d))),
            pl.BlockSpec((tm, GDN_W), lambda i: (mm(i), 0)),
            resident((SWA_Q_W, d)),
            resident((GDN_W, d)),
            resident((XA_W, d)),
            resident((d, d)),
            row_slab(w1),
            row_slab(w2),
        ],
        out_specs=(pl.BlockSpec((tm, d), lambda i: (mm(i), 0)), row_slab(w1), row_slab(w2)),
        scratch_shapes=[pltpu.VMEM((tm, SWA_Q_W + XA_W), BF16), pltpu.VMEM((tm, SWA_Q_W + XA_W), BF16)],
        compiler_params=pltpu.CompilerParams(
            dimension_semantics=("arbitrary",), vmem_limit_bytes=VMEM_LIMIT),
        name="mix",
    )(sinks, p, p, p, p, mkv, x, p, yb, wa, wb, wc, wo, w1, w2)


def _mlp_kernel(h_ref, g_ref, w1_ref, w2_ref, gf_ref, o_ref, n_ref, acc_ref, *, final_norm):
    j = pl.program_id(1)

    @pl.when(j == 0)
    def _():
        h = h_ref[...]
        n_ref[...] = _rms(h, g_ref[...]).astype(BF16)
        acc_ref[...] = h

    u = _dot(n_ref[...], w1_ref[...])
    acc_ref[...] += _dot(jnp.square(jnp.maximum(u, 0.0)).astype(BF16), w2_ref[...])

    @pl.when(j == pl.num_programs(1) - 1)
    def _():
        h = acc_ref[...]
        o_ref[...] = _rms(h, gf_ref[...]) if final_norm else h


def _mlp(h, g, w1, w2, g_final, *, final_norm, tm=512, tf=1024):
    t, d = h.shape
    f = w1.shape[1]
    return pl.pallas_call(
        functools.partial(_mlp_kernel, final_norm=final_norm),
        out_shape=jax.ShapeDtypeStruct((t, d), F32),
        grid=(t // tm, f // tf),
        in_specs=[
            pl.BlockSpec((tm, d), lambda i, j: (i, 0)),
            pl.BlockSpec((1, d), lambda i, j: (0, 0)),
            pl.BlockSpec((d, tf), lambda i, j: (0, j)),
            pl.BlockSpec((tf, d), lambda i, j: (j, 0)),
            pl.BlockSpec((1, d), lambda i, j: (0, 0)),
        ],
        out_specs=pl.BlockSpec((tm, d), lambda i, j: (i, 0)),
        scratch_shapes=[pltpu.VMEM((tm, d), BF16), pltpu.VMEM((tm, d), F32)],
        compiler_params=pltpu.CompilerParams(
            dimension_semantics=("arbitrary", "arbitrary"), vmem_limit_bytes=VMEM_LIMIT),
        name="mlp",
    )(h, g, w1, w2, g_final)


def _lane_row(v):
    return jnp.zeros((1, LANES), F32).at[0, :v.shape[0]].set(v.astype(F32))


def kernel(x, mem, g_mix, w_in, sinks, conv_w, a_log, dt_bias, gdn_norm_w, g_mem, w_mem_kv, w_swa_up,
           w_gdn_up, w_xa_up, w_out, g_mlp, w_mlp_in, w_mlp_out, g_final):
    batch, seq, d = x.shape
    depth = w_in.shape[0]
    h = x.reshape(batch * seq, d)
    mem2 = mem.reshape(batch * N_MEM, d)
    for l in range(depth):
        p, ab = _in_proj(h, g_mix[l][None], jnp.swapaxes(w_in[l], 0, 1))
        mkv = _mem_kv(mem2, g_mem[l][None], w_mem_kv[l])
        yb, wo, wa, wb, wc = _gdn(p, ab, conv_w[l], _lane_row(a_log[l]), _lane_row(dt_bias[l]),
                                  gdn_norm_w[l][None], (w_out[l], w_swa_up[l], w_gdn_up[l], w_xa_up[l]),
                                  batch, seq)
        h, w1, w2 = _mix(h, p, mkv, yb, sinks[l], wa, wb, wc, wo, w_mlp_in[l], w_mlp_out[l], seq)
        h = _mlp(h, g_mlp[l][None], w1, w2, g_final[None], final_norm=(l == depth - 1))
    return h.reshape(batch, seq, d)
```

```python
import functools

import jax
import jax.numpy as jnp
from jax import lax
from jax.experimental import pallas as pl
from jax.experimental.pallas import tpu as pltpu

F32 = jnp.float32
BF16 = jnp.bfloat16

D_MODEL = 2048
SWA_Q_HEADS = 16
SWA_KV_HEADS = 2
SWA_HEAD_DIM = 64
SWA_WINDOW = 128
GDN_HEADS = 4
GDN_HEAD_DIM = 128
GDN_CONV = 4
GDN_CHUNK = 64
N_MEM = 256
XA_HEADS = 4
XA_HEAD_DIM = 128
D_FF = 4 * D_MODEL
RMS_EPS = 1e-6
L2_EPS = 1e-6

SWA_Q_W = SWA_Q_HEADS * SWA_HEAD_DIM
SWA_KV_W = SWA_KV_HEADS * SWA_HEAD_DIM
GDN_W = GDN_HEADS * GDN_HEAD_DIM
XA_W = XA_HEADS * XA_HEAD_DIM

LANES = 128
SUBLANES = 8
BF16_SUBLANES = 16

AB_W = 2 * GDN_HEADS

R_QA = 0
R_KA = R_QA + SWA_Q_W
R_VA = R_KA + SWA_KV_W
R_QB = R_VA + SWA_KV_W
R_AB = R_QB + 3 * GDN_W
R_Z = R_AB + AB_W
R_QC = R_Z + GDN_W
R_GATE = R_QC + XA_W
R_END = R_GATE + 3 * D_MODEL

IN_PROJ_TN = 1024
IN_PROJ_TILE_STARTS = (
    tuple(R_GATE + k * IN_PROJ_TN for k in range(3 * D_MODEL // IN_PROJ_TN))
    + tuple(range(0, R_Z, IN_PROJ_TN))
    + tuple(R_Z + k * IN_PROJ_TN for k in range((R_GATE - R_Z) // IN_PROJ_TN)))
P_WIDTH = len(IN_PROJ_TILE_STARTS) * IN_PROJ_TN
P_GATE = 0
P_FRONT = 3 * D_MODEL
P_QA = P_FRONT + R_QA
P_KA = P_FRONT + R_KA
P_VA = P_FRONT + R_VA
P_QB = P_FRONT + R_QB
P_AB = P_FRONT + R_AB
P_Z = P_FRONT + -(-R_Z // IN_PROJ_TN) * IN_PROJ_TN
P_QC = P_Z + GDN_W
assert (R_GATE - R_Z) % IN_PROJ_TN == 0 and (3 * D_MODEL) % IN_PROJ_TN == 0
assert all(s % SUBLANES == 0 for s in IN_PROJ_TILE_STARTS) and P_AB % LANES == 0 and P_QC + XA_W == P_WIDTH

VMEM_LIMIT = 56 * 1024 * 1024
IN_PROJ_VMEM_LIMIT = 60 * 1024 * 1024


def _rms(x, g):
    return x * lax.rsqrt(jnp.mean(x * x, axis=-1, keepdims=True) + RMS_EPS) * g


def _dot(a, b):
    return jnp.dot(a, b, preferred_element_type=F32)


def _dot_nt(a, b):
    return lax.dot_general(a, b, (((1,), (1,)), ((), ())), preferred_element_type=F32)


def _dot_f32(a, b):
    return jnp.dot(a, b, preferred_element_type=F32, precision=lax.Precision.HIGHEST)


W_RING = 3


def _in_proj_kernel(starts_ref, x_ref, g_ref, wt_hbm, o_ref, ab_ref, n_ref, wbuf_ref, sem_ref, *, ab_tile, ab_off):
    j = pl.program_id(1)
    n_col = pl.num_programs(1)
    n_steps = pl.num_programs(0) * n_col
    s = pl.program_id(0) * n_col + j
    tn = o_ref.shape[1]

    def tile_copy(step):
        start = pl.multiple_of(starts_ref[step % n_col], SUBLANES)
        slot = step % W_RING
        return pltpu.make_async_copy(wt_hbm.at[pl.ds(start, tn), :], wbuf_ref.at[slot], sem_ref.at[slot])

    @pl.when(s == 0)
    def _():
        for first in range(W_RING - 1):
            tile_copy(first).start()

    @pl.when(s + W_RING - 1 < n_steps)
    def _():
        tile_copy(s + W_RING - 1).start()

    @pl.when(j == 0)
    def _():
        n_ref[...] = _rms(x_ref[...], g_ref[...]).astype(BF16)

    tile_copy(s).wait()
    acc = _dot_nt(n_ref[...], wbuf_ref[s % W_RING].astype(BF16))
    o_ref[...] = acc.astype(o_ref.dtype)

    @pl.when(j == ab_tile)
    def _():
        ab_ref[...] = acc[:, ab_off:ab_off + LANES]


def _in_proj(x, g, wt, *, tm=1024, tn=IN_PROJ_TN):
    t, d = x.shape
    return pl.pallas_call(
        functools.partial(_in_proj_kernel, ab_tile=P_AB // tn, ab_off=P_AB % tn),
        out_shape=(jax.ShapeDtypeStruct((t, P_WIDTH), BF16), jax.ShapeDtypeStruct((t, LANES), F32)),
        grid_spec=pltpu.PrefetchScalarGridSpec(
            num_scalar_prefetch=1,
            grid=(t // tm, P_WIDTH // tn),
            in_specs=[
                pl.BlockSpec((tm, d), lambda i, j, starts: (i, 0)),
                pl.BlockSpec((1, d), lambda i, j, starts: (0, 0)),
                pl.BlockSpec(memory_space=pl.ANY),
            ],
            out_specs=(pl.BlockSpec((tm, tn), lambda i, j, starts: (i, j)),
                       pl.BlockSpec((tm, LANES), lambda i, j, starts: (i, 0))),
            scratch_shapes=[pltpu.VMEM((tm, d), BF16), pltpu.VMEM((W_RING, tn, d), F32),
                            pltpu.SemaphoreType.DMA((W_RING,))],
        ),
        compiler_params=pltpu.CompilerParams(
            dimension_semantics=("arbitrary", "arbitrary"), vmem_limit_bytes=IN_PROJ_VMEM_LIMIT),
        name="in_proj",
    )(jnp.asarray(IN_PROJ_TILE_STARTS, jnp.int32), x, g, wt)


def _mem_kv_kernel(m_ref, g_ref, w_ref, o_ref):
    o_ref[...] = _dot(_rms(m_ref[...], g_ref[...]).astype(BF16), w_ref[...].astype(BF16)).astype(o_ref.dtype)


def _mem_kv(mem, g, w):
    t, d = mem.shape
    n = w.shape[1]
    return pl.pallas_call(
        _mem_kv_kernel,
        out_shape=jax.ShapeDtypeStruct((t, n), BF16),
        grid=(t // N_MEM,),
        in_specs=[
            pl.BlockSpec((N_MEM, d), lambda i: (i, 0)),
            pl.BlockSpec((1, d), lambda i: (0, 0)),
            pl.BlockSpec((d, n), lambda i: (0, 0)),
        ],
        out_specs=pl.BlockSpec((N_MEM, n), lambda i: (i, 0)),
        compiler_params=pltpu.CompilerParams(
            dimension_semantics=("arbitrary",), vmem_limit_bytes=VMEM_LIMIT),
        name="mem_kv",
    )(mem, g, w)


def _swa_block(sinks_ref, q_ref, rows, k, v, first, o_ref):
    w = SWA_WINDOW
    hd = SWA_HEAD_DIM
    group = SWA_Q_HEADS // SWA_KV_HEADS
    ones = jnp.ones((2 * w, hd), BF16)
    qi = lax.broadcasted_iota(jnp.int32, (w, 2 * w), 0)
    kj = lax.broadcasted_iota(jnp.int32, (w, 2 * w), 1)
    valid = (kj > qi) & (kj <= qi + w)
    if first is not False:
        valid = valid & (kj >= jnp.where(first, w, 0))
    heads = range(SWA_Q_HEADS)
    kh = [k[:, hk * hd:(hk + 1) * hd] for hk in range(SWA_KV_HEADS)]
    vh = [jnp.concatenate([v[:, hk * hd:(hk + 1) * hd], ones], axis=1) for hk in range(SWA_KV_HEADS)]
    s = [_dot_nt(q_ref[rows, h * hd:(h + 1) * hd], kh[h // group]) for h in heads]
    s = [jnp.where(valid, s[h], -jnp.inf) for h in heads]
    m = [jnp.maximum(jnp.max(s[h], axis=-1, keepdims=True), sinks_ref[h]) for h in heads]
    p = [jnp.exp(s[h] - m[h]).astype(BF16) for h in heads]
    ov = [_dot(p[h], vh[h // group]) for h in heads]
    for h in heads:
        o = ov[h][:, :hd] / (ov[h][:, hd:hd + 1] + jnp.exp(sinks_ref[h] - m[h]))
        o_ref[rows, h * hd:(h + 1) * hd] = o.astype(o_ref.dtype)


XA_SUB = 128


def _xattn_rows(q_ref, mkv_ref, o_ref, col0):
    hd = XA_HEAD_DIM
    ones = jnp.ones((N_MEM, hd), BF16)
    units = [(slice(r, r + XA_SUB), h) for r in range(0, q_ref.shape[0], XA_SUB) for h in range(XA_HEADS)]
    mv = [jnp.concatenate([mkv_ref[:, XA_W + h * hd:XA_W + (h + 1) * hd], ones], axis=1) for h in range(XA_HEADS)]
    s = [_dot_nt(q_ref[rs, h * hd:(h + 1) * hd], mkv_ref[:, h * hd:(h + 1) * hd]) * (hd ** -0.5)
         for rs, h in units]
    e = [jnp.exp(si - jnp.max(si, axis=-1, keepdims=True)).astype(BF16) for si in s]
    oe = [_dot(ei, mv[h]) for ei, (_, h) in zip(e, units)]
    for (rs, h), oi in zip(units, oe):
        o_ref[rs, col0 + h * hd:col0 + (h + 1) * hd] = (oi[:, :hd] / oi[:, hd:hd + 1]).astype(o_ref.dtype)


GDN_TS = 256
GDN_QKV_BLOCK = 256
GDN_QKV_BLOCKS = 3 * GDN_W // GDN_QKV_BLOCK
assert P_QB % GDN_QKV_BLOCK == 0
CONV_PAD = SUBLANES


GDN_CAST_WEIGHTS = 4


def _gdn_kernel(*refs):
    qkv_refs = refs[:GDN_QKV_BLOCKS]
    refs = refs[GDN_QKV_BLOCKS:]
    z_ref, ab_ref, cw_ref, alog_ref, dtb_ref, nw_ref = refs[:6]
    wf_refs = refs[6:6 + GDN_CAST_WEIGHTS]
    y_ref = refs[6 + GDN_CAST_WEIGHTS]
    wb_refs = refs[7 + GDN_CAST_WEIGHTS:7 + 2 * GDN_CAST_WEIGHTS]
    state_ref, carry_ref, xpad_ref, cv_ref = refs[7 + 2 * GDN_CAST_WEIGHTS:]
    for wf_ref, wb_ref in zip(wf_refs, wb_refs):
        wb_ref[...] = wf_ref[...].astype(BF16)
    ts = GDN_TS
    c = GDN_CHUNK
    dh = GDN_HEAD_DIM
    gw = GDN_W
    heads = range(GDN_HEADS)
    chunks = [slice(i * c, (i + 1) * c) for i in range(ts // c)]

    @pl.when(pl.program_id(1) == 0)
    def _():
        state_ref[...] = jnp.zeros_like(state_ref)
        carry_ref[...] = jnp.zeros_like(carry_ref)

    xpad_ref[0:CONV_PAD, :] = carry_ref[...]
    for i, r in enumerate(qkv_refs):
        xpad_ref[CONV_PAD:, i * GDN_QKV_BLOCK:(i + 1) * GDN_QKV_BLOCK] = r[...].astype(F32)
    carry_ref[...] = xpad_ref[ts:ts + CONV_PAD, :]
    for cb in range(3 * gw // LANES):
        cs = slice(cb * LANES, (cb + 1) * LANES)
        acc = None
        for i in range(GDN_CONV):
            off = CONV_PAD - (GDN_CONV - 1) + i
            term = cw_ref[i:i + 1, cs] * xpad_ref[off:off + ts, cs]
            acc = term if acc is None else acc + term
        cv_ref[:, cs] = acc * jax.nn.sigmoid(acc)

    ab = ab_ref[...]
    g = -jnp.exp(alog_ref[...]) * jax.nn.softplus(ab + dtb_ref[...])
    beta_all = jax.nn.sigmoid(ab)
    ri = lax.broadcasted_iota(jnp.int32, (ts, ts), 0)
    ci = lax.broadcasted_iota(jnp.int32, (ts, ts), 1)
    same_chunk = (ri // c) == (ci // c)
    causal = same_chunk & (ri >= ci)
    strict = same_chunk & (ri > ci)
    gcum = _dot_f32(causal.astype(F32), g)
    gcum_t = gcum.T
    g_last = jnp.concatenate(
        [jnp.broadcast_to(gcum[rc.stop - 1:rc.stop], (c, LANES)) for rc in chunks], axis=0)
    k_scale = jnp.exp(g_last - gcum)
    eg_all = jnp.exp(gcum)

    q, k, v, gc, beta = [], [], [], [], []
    for h in heads:
        qh = cv_ref[:, h * dh:(h + 1) * dh]
        kh = cv_ref[:, gw + h * dh:gw + (h + 1) * dh]
        q.append(qh * lax.rsqrt(jnp.sum(qh * qh, axis=-1, keepdims=True) + L2_EPS) * (dh ** -0.5))
        k.append(kh * lax.rsqrt(jnp.sum(kh * kh, axis=-1, keepdims=True) + L2_EPS))
        v.append(cv_ref[:, 2 * gw + h * dh:2 * gw + (h + 1) * dh])
        gc.append(gcum[:, h:h + 1])
        beta.append(beta_all[:, GDN_HEADS + h:GDN_HEADS + h + 1])
    qkk = [_dot_nt(jnp.concatenate([q[h], k[h]], axis=0).astype(BF16), k[h].astype(BF16)) for h in heads]
    decay = [jnp.exp(jnp.where(causal, gc[h] - gcum_t[h:h + 1, :], -jnp.inf)) for h in heads]
    qk = [qkk[h][:ts] * decay[h] for h in heads]
    m = [-jnp.where(strict, beta[h] * qkk[h][ts:] * decay[h], 0.0) for h in heads]
    sol = [jnp.concatenate([v[h] * beta[h], k[h] * (beta[h] * eg_all[:, h:h + 1])], axis=-1) for h in heads]
    n_rounds = c.bit_length() - 1
    for r in range(n_rounds):
        last = r + 1 == n_rounds
        mb = [m[h].astype(BF16) for h in heads]
        rhs = [sol[h].astype(BF16) if last else jnp.concatenate([sol[h], m[h]], axis=-1).astype(BF16)
               for h in heads]
        prod = [_dot(mb[h], rhs[h]) for h in heads]
        sol = [sol[h] + prod[h][:, :2 * dh] for h in heads]
        if not last:
            m = [prod[h][:, 2 * dh:] for h in heads]
    k_dec = [k[h] * k_scale[:, h:h + 1] for h in heads]
    q_dec = [q[h] * eg_all[:, h:h + 1] for h in heads]

    x = {}
    for ic, rc in enumerate(chunks):
        for h in heads:
            lhs = jnp.concatenate([k_dec[h][rc].T, qk[h][rc, rc]], axis=0).astype(BF16)
            x[ic, h] = _dot(lhs, sol[h][rc].astype(BF16))

    for ic, rc in enumerate(chunks):
        for h in heads:
            xs = x[ic, h]
            lhs = jnp.concatenate([-xs[:dh, dh:], q_dec[h][rc] - xs[dh:, dh:]], axis=0).astype(BF16)
            s = state_ref[h]
            y = _dot(lhs, s.astype(BF16))
            state_ref[h] = s * eg_all[rc.stop - 1:rc.stop, h:h + 1] + y[:dh] + xs[:dh, :dh]
            o = _rms(y[dh:] + xs[dh:, :dh], nw_ref[...])
            hs = slice(h * dh, (h + 1) * dh)
            z = z_ref[rc, hs].astype(F32)
            y_ref[rc, hs] = (o * (z * jax.nn.sigmoid(z))).astype(y_ref.dtype)


def _gdn(p, ab, conv_w, alog_row, dtb_row, norm_w, cast_weights, batch, seq):
    t = p.shape[0]
    ts = GDN_TS
    nj = seq // ts
    gw = GDN_W
    n_steps = batch * nj
    assert len(cast_weights) == GDN_CAST_WEIGHTS

    def rows(col_blk):
        return lambda b, j: (b * nj + j, col_blk)

    def row_slab(wf):
        r, c = wf.shape
        slab = max(r // n_steps, BF16_SUBLANES)
        assert r % slab == 0 and n_steps % (r // slab) == 0
        reuse = n_steps // (r // slab)
        return pl.BlockSpec((slab, c), lambda b, j: ((b * nj + j) // reuse, 0))

    w_specs = [row_slab(wf) for wf in cast_weights]
    return pl.pallas_call(
        _gdn_kernel,
        out_shape=(jax.ShapeDtypeStruct((t, gw), BF16), *[jax.ShapeDtypeStruct(wf.shape, BF16) for wf in cast_weights]),
        grid=(batch, nj),
        in_specs=[
            *[pl.BlockSpec((ts, GDN_QKV_BLOCK), rows(P_QB // GDN_QKV_BLOCK + i)) for i in range(GDN_QKV_BLOCKS)],
            pl.BlockSpec((ts, gw), rows(P_Z // gw)),
            pl.BlockSpec((ts, LANES), rows(0)),
            pl.BlockSpec((GDN_CONV, 3 * gw), lambda b, j: (0, 0)),
            pl.BlockSpec((1, LANES), lambda b, j: (0, 0)),
            pl.BlockSpec((1, LANES), lambda b, j: (0, 0)),
            pl.BlockSpec((1, GDN_HEAD_DIM), lambda b, j: (0, 0)),
            *w_specs,
        ],
        out_specs=(pl.BlockSpec((ts, gw), rows(0)), *w_specs),
        scratch_shapes=[
            pltpu.VMEM((GDN_HEADS, GDN_HEAD_DIM, GDN_HEAD_DIM), F32),
            pltpu.VMEM((CONV_PAD, 3 * gw), F32),
            pltpu.VMEM((ts + CONV_PAD, 3 * gw), F32),
            pltpu.VMEM((ts, 3 * gw), F32),
        ],
        compiler_params=pltpu.CompilerParams(dimension_semantics=("arbitrary", "arbitrary")),
        name="gdn",
    )(*([p] * GDN_QKV_BLOCKS), p, ab, conv_w, alog_row, dtb_row, norm_w, *cast_weights)


def _mix_kernel(sinks_ref, q_ref, kv_ref, kvp_ref, qc_ref, mkv_ref, x_ref, gate_ref, yb_ref,
                wa_ref, wb_ref, wc_ref, wo_ref, w1_ref, w2_ref, h_ref, w1b_ref, w2b_ref, y0_ref, y1_ref, *,
                tiles_per_seq, n_tiles):
    i = pl.program_id(0)
    w = SWA_WINDOW
    hd = SWA_HEAD_DIM

    @pl.when(i == 0)
    def _():
        y0_ref[...] = jnp.zeros_like(y0_ref)

    def step(yprev_ref, ynext_ref):
        w1b_ref[...] = w1_ref[...].astype(BF16)
        w2b_ref[...] = w2_ref[...].astype(BF16)

        first = (jnp.minimum(i, n_tiles - 1) % tiles_per_seq) == 0
        kvw = SWA_KV_W
        k_all = (jnp.concatenate([kvp_ref[:, :kvw], kv_ref[:, :kvw]], axis=0).astype(F32)
                 * (hd ** -0.5)).astype(BF16)
        v_all = jnp.concatenate([kvp_ref[:, kvw:], kv_ref[:, kvw:]], axis=0)
        for b in range(q_ref.shape[0] // w):
            _swa_block(sinks_ref, q_ref, slice(b * w, (b + 1) * w), k_all[b * w:(b + 2) * w],
                       v_all[b * w:(b + 2) * w], first if b == 0 else False, ynext_ref)
        _xattn_rows(qc_ref, mkv_ref, ynext_ref, SWA_Q_W)

        d = x_ref.shape[1]
        gate = [jax.nn.sigmoid(gate_ref[:, b * d:(b + 1) * d].astype(F32)) for b in range(3)]
        merged = gate[0] * _dot(yprev_ref[:, :SWA_Q_W], wa_ref[...])
        merged += gate[1] * _dot(yb_ref[...], wb_ref[...])
        merged += gate[2] * _dot(yprev_ref[:, SWA_Q_W:], wc_ref[...])
        h_ref[...] = x_ref[...] + _dot(merged.astype(BF16), wo_ref[...])

    @pl.when(i % 2 == 0)
    def _():
        step(y0_ref, y1_ref)

    @pl.when(i % 2 == 1)
    def _():
        step(y1_ref, y0_ref)


def _mix(x, p, mkv, yb, sinks, wa, wb, wc, wo, w1, w2, seq, *, tm=256):
    t, d = x.shape
    w = SWA_WINDOW
    n_tiles = t // tm
    tiles_per_seq = seq // tm
    kv_w = 2 * SWA_KV_W
    assert P_VA == P_KA + SWA_KV_W and P_KA % kv_w == 0

    def att(i):
        return jnp.minimum(i, n_tiles - 1)

    def mm(i):
        return jnp.maximum(i - 1, 0)

    def resident(shape):
        return pl.BlockSpec(shape, lambda i: (0, 0), pipeline_mode=pl.Buffered(1))

    def row_slab(wf):
        r, c = wf.shape
        assert r % (n_tiles * BF16_SUBLANES) == 0
        return pl.BlockSpec((r // n_tiles, c), lambda i: (att(i), 0))

    return pl.pallas_call(
        functools.partial(_mix_kernel, tiles_per_seq=tiles_per_seq, n_tiles=n_tiles),
        out_shape=(jax.ShapeDtypeStruct((t, d), F32), jax.ShapeDtypeStruct(w1.shape, BF16),
                   jax.ShapeDtypeStruct(w2.shape, BF16)),
        grid=(n_tiles + 1,),
        in_specs=[
            pl.BlockSpec(memory_space=pltpu.SMEM),
            pl.BlockSpec((tm, SWA_Q_W), lambda i: (att(i), P_QA // SWA_Q_W)),
            pl.BlockSpec((tm, kv_w), lambda i: (att(i), P_KA // kv_w)),
            pl.BlockSpec((w, kv_w), lambda i: (jnp.maximum(att(i) * (tm // w) - 1, 0), P_KA // kv_w)),
            pl.BlockSpec((tm, XA_W), lambda i: (att(i), P_QC // XA_W)),
            pl.BlockSpec((N_MEM, 2 * XA_W), lambda i: (att(i) // tiles_per_seq, 0)),
            pl.BlockSpec((tm, d), lambda i: (mm(i), 0)),
            pl.BlockSpec((tm, 3 * d), lambda i: (mm(i), P_GATE // (3 * d))),
            pl.BlockSpec((tm, GDN_W), lambda i: (mm(i), 0)),
            resident((SWA_Q_W, d)),
            resident((GDN_W, d)),
            resident((XA_W, d)),
            resident((d, d)),
            row_slab(w1),
            row_slab(w2),
        ],
        out_specs=(pl.BlockSpec((tm, d), lambda i: (mm(i), 0)), row_slab(w1), row_slab(w2)),
        scratch_shapes=[pltpu.VMEM((tm, SWA_Q_W + XA_W), BF16), pltpu.VMEM((tm, SWA_Q_W + XA_W), BF16)],
        compiler_params=pltpu.CompilerParams(
            dimension_semantics=("arbitrary",), vmem_limit_bytes=VMEM_LIMIT),
        name="mix",
    )(sinks, p, p, p, p, mkv, x, p, yb, wa, wb, wc, wo, w1, w2)


def _mlp_kernel(h_ref, g_ref, w1_hbm, w2_hbm, gf_ref, o_ref, n_ref, acc_ref, w1buf_ref, w2buf_ref, sem1_ref,
                sem2_ref, *, final_norm):
    j = pl.program_id(1)
    n_col = pl.num_programs(1)
    n_steps = pl.num_programs(0) * n_col
    s = pl.program_id(0) * n_col + j
    tf = w1buf_ref.shape[2]

    def tile_copies(step):
        c0 = pl.multiple_of((step % n_col) * tf, tf)
        slot = step % W_RING
        return (pltpu.make_async_copy(w1_hbm.at[:, pl.ds(c0, tf)], w1buf_ref.at[slot], sem1_ref.at[slot]),
                pltpu.make_async_copy(w2_hbm.at[pl.ds(c0, tf), :], w2buf_ref.at[slot], sem2_ref.at[slot]))

    @pl.when(s == 0)
    def _():
        for first in range(W_RING - 1):
            for copy in tile_copies(first):
                copy.start()

    @pl.when(s + W_RING - 1 < n_steps)
    def _():
        for copy in tile_copies(s + W_RING - 1):
            copy.start()

    @pl.when(j == 0)
    def _():
        h = h_ref[...]
        n_ref[...] = _rms(h, g_ref[...]).astype(BF16)
        acc_ref[...] = h

    for copy in tile_copies(s):
        copy.wait()
    u = _dot(n_ref[...], w1buf_ref[s % W_RING])
    acc_ref[...] += _dot(jnp.square(jnp.maximum(u, 0.0)).astype(BF16), w2buf_ref[s % W_RING])

    @pl.when(j == pl.num_programs(1) - 1)
    def _():
        h = acc_ref[...]
        o_ref[...] = _rms(h, gf_ref[...]) if final_norm else h


def _mlp(h, g, w1, w2, g_final, *, final_norm, tm=512, tf=1024):
    t, d = h.shape
    f = w1.shape[1]
    return pl.pallas_call(
        functools.partial(_mlp_kernel, final_norm=final_norm),
        out_shape=jax.ShapeDtypeStruct((t, d), F32),
        grid=(t // tm, f // tf),
        in_specs=[
            pl.BlockSpec((tm, d), lambda i, j: (i, 0)),
            pl.BlockSpec((1, d), lambda i, j: (0, 0)),
            pl.BlockSpec(memory_space=pl.ANY),
            pl.BlockSpec(memory_space=pl.ANY),
            pl.BlockSpec((1, d), lambda i, j: (0, 0)),
        ],
        out_specs=pl.BlockSpec((tm, d), lambda i, j: (i, 0)),
        scratch_shapes=[pltpu.VMEM((tm, d), BF16), pltpu.VMEM((tm, d), F32),
                        pltpu.VMEM((W_RING, d, tf), BF16), pltpu.VMEM((W_RING, tf, d), BF16),
                        pltpu.SemaphoreType.DMA((W_RING,)), pltpu.SemaphoreType.DMA((W_RING,))],
        compiler_params=pltpu.CompilerParams(
            dimension_semantics=("arbitrary", "arbitrary"), vmem_limit_bytes=VMEM_LIMIT),
        name="mlp",
    )(h, g, w1, w2, g_final)


def _lane_row(v):
    return jnp.zeros((1, LANES), F32).at[0, :v.shape[0]].set(v.astype(F32))


def kernel(x, mem, g_mix, w_in, sinks, conv_w, a_log, dt_bias, gdn_norm_w, g_mem, w_mem_kv, w_swa_up,
           w_gdn_up, w_xa_up, w_out, g_mlp, w_mlp_in, w_mlp_out, g_final):
    batch, seq, d = x.shape
    depth = w_in.shape[0]
    h = x.reshape(batch * seq, d)
    mem2 = mem.reshape(batch * N_MEM, d)
    for l in range(depth):
        p, ab = _in_proj(h, g_mix[l][None], jnp.swapaxes(w_in[l], 0, 1))
        mkv = _mem_kv(mem2, g_mem[l][None], w_mem_kv[l])
        yb, wo, wa, wb, wc = _gdn(p, ab, conv_w[l], _lane_row(a_log[l]), _lane_row(dt_bias[l]),
                                  gdn_norm_w[l][None], (w_out[l], w_swa_up[l], w_gdn_up[l], w_xa_up[l]),
                                  batch, seq)
        h, w1, w2 = _mix(h, p, mkv, yb, sinks[l], wa, wb, wc, wo, w_mlp_in[l], w_mlp_out[l], seq)
        h = _mlp(h, g_mlp[l][None], w1, w2, g_final[None], final_norm=(l == depth - 1))
    return h.reshape(batch, seq, d)
```
